```python
import math
import jax
import jax.numpy as jnp
from jax import lax
import numpy as np

D_MODEL = 1024
BATCH = 16
SEQ = 4096
DEPTH = 2
DEC_BATCH = 2
DEC_SEQ = 16384
PAST_LEN = 128

HEAD_DIM = 64
A_Q_HEADS = 8
A_KV_HEADS = 2
A_GROUP = A_Q_HEADS // A_KV_HEADS
A_HALF_WINDOW = 128
B_PAIRS = ((128, 1), (512, 4), (2048, 16))
N_DIL = len(B_PAIRS)
B_HEADS = 4
C_HEADS = 8
GRID_W = 64
NA_KH = 8
NA_KW = 16
D_HEADS = 4
D_HEAD_DIM = 128
D_CHUNK = 64
T5_BUCKETS = 32
T5_MAX_DISTANCE = 1024
T5_HEADS = A_Q_HEADS + N_DIL * B_HEADS
ALPHA = (2.0 * DEPTH) ** 0.25
BETA = (8.0 * DEPTH) ** -0.25
N_EVEN = (DEPTH + 1) // 2
N_ODD = DEPTH // 2
LN_EPS = 1e-5
RMS_EPS = 1e-6

A_Q_W = A_Q_HEADS * HEAD_DIM
A_KV_W = A_KV_HEADS * HEAD_DIM
B_W = B_HEADS * HEAD_DIM
B_QKV_W = N_DIL * B_W
EVEN_SPLITS = (A_Q_W, A_KV_W, A_KV_W, B_QKV_W, B_QKV_W, B_QKV_W, A_Q_W, B_W)
EVEN_IN = sum(EVEN_SPLITS)
EVEN_OUT = A_Q_W + B_W
C_W = C_HEADS * HEAD_DIM
D_W = D_HEADS * D_HEAD_DIM
ODD_SPLITS = (C_W, C_W, C_W, D_W, D_W, D_W, D_W, C_W, D_W)
ODD_IN = sum(ODD_SPLITS)
ODD_OUT = C_W + D_W

kernel_name = "hybrid_bidir_encoder_swa_dilated_na_hgrn2"


def _split(h, sizes):
    return jnp.split(h, [int(c) for c in np.cumsum(sizes)[:-1]], axis=-1)


def t5_buckets(rel):
    nb = T5_BUCKETS // 2
    ret = (rel > 0).astype(np.int64) * nb
    n = np.abs(rel)
    max_exact = nb // 2
    large = max_exact + (np.log(np.maximum(n, 1) / max_exact) / math.log(T5_MAX_DISTANCE / max_exact) * (nb - max_exact)).astype(np.int64)
    large = np.minimum(large, nb - 1)
    return ret + np.where(n < max_exact, n, large)


def layer_norm(x, g, b):
    xf = x.astype(jnp.float32)
    mu = xf.mean(-1, keepdims=True)
    var = jnp.square(xf - mu).mean(-1, keepdims=True)
    return ((xf - mu) * lax.rsqrt(var + LN_EPS) * g.astype(jnp.float32) + b.astype(jnp.float32)).astype(x.dtype)


def banded_attention(q, k, v, bias, sink):
    nb, L, hk, g, hd = q.shape
    W = bias.shape[-2]
    n = -(-L // W)
    pad = n * W - L
    qb = jnp.pad(q, ((0, 0), (0, pad), (0, 0), (0, 0), (0, 0))).reshape(nb, n, W, hk, g, hd)

    def windows(t):
        tp = jnp.pad(t, ((0, 0), (W, pad + W), (0, 0), (0, 0))).reshape(nb, n + 2, W, hk, hd)
        return jnp.concatenate([tp[:, :-2], tp[:, 1:-1], tp[:, 2:]], axis=2)

    kw, vw = windows(k), windows(v)
    qpos = np.arange(n)[:, None, None] * W + np.arange(W)[None, :, None]
    kpos = np.arange(n)[:, None, None] * W + np.arange(3 * W)[None, None, :] - W
    valid = (kpos >= 0) & (kpos < L) & (np.abs(kpos - qpos) <= W)
    s = jnp.einsum('bnqhgd,bnkhd->bnhgqk', qb, kw).astype(jnp.float32) * (hd ** -0.5) + bias
    s = jnp.where(valid[None, :, None, None], s, -jnp.inf)
    m = s.max(-1)
    if sink is not None:
        m = jnp.maximum(m, sink[:, :, None])
    p = jnp.exp(s - m[..., None])
    denom = p.sum(-1)
    if sink is not None:
        denom = denom + jnp.exp(sink[:, :, None] - m)
    o = jnp.einsum('bnhgqk,bnkhd->bnqhgd', p, vw.astype(jnp.float32))
    denom_q = jnp.moveaxis(denom, -1, 2)
    o = (o / denom_q[..., None]).reshape(nb, n * W, hk, g, hd)[:, :L]
    lse = (jnp.moveaxis(m, -1, 2) + jnp.log(denom_q)).reshape(nb, n * W, hk, g)[:, :L]
    return o, lse


def _to_dilated(t, d):
    nb, T = t.shape[:2]
    t = t.reshape(nb, T // d, d, *t.shape[2:])
    return jnp.moveaxis(t, 2, 1).reshape(nb * d, T // d, *t.shape[3:])


def _from_dilated(t, nb, d):
    ld = t.shape[1]
    t = t.reshape(nb, d, ld, *t.shape[2:])
    return jnp.moveaxis(t, 1, 2).reshape(nb, ld * d, *t.shape[3:])


def neighbourhood_attention(q, k, v, rpb):
    nb, T, H, hd = q.shape
    rows = T // GRID_W
    kh = min(NA_KH, rows)
    qg = q.reshape(nb, rows, GRID_W, H, hd)
    kg = k.reshape(nb, rows, GRID_W, H, hd)
    vg = v.reshape(nb, rows, GRID_W, H, hd)
    qc = np.arange(GRID_W)
    sc = np.clip(qc - NA_KW // 2, 0, GRID_W - NA_KW)
    col_mask = (qc[None, :] >= sc[:, None]) & (qc[None, :] < sc[:, None] + NA_KW)
    col_idx = np.clip(qc[None, :] - qc[:, None] + NA_KW - 1, 0, 2 * NA_KW - 2)
    rpb = rpb.astype(jnp.float32)

    def row_block(r):
        sr = jnp.clip(r - kh // 2, 0, rows - kh)
        ks = lax.dynamic_slice_in_dim(kg, sr, kh, axis=1)
        vs = lax.dynamic_slice_in_dim(vg, sr, kh, axis=1)
        qr = lax.dynamic_index_in_dim(qg, r, axis=1, keepdims=False)
        dr = sr + jnp.arange(kh) - r
        bias = jnp.take(rpb, dr + NA_KH - 1, axis=1)[:, :, col_idx]
        s = jnp.einsum('bqhd,bikhd->bhqik', qr, ks).astype(jnp.float32) * (hd ** -0.5) + jnp.transpose(bias, (0, 2, 1, 3))
        s = jnp.where(col_mask[None, None, :, None, :], s, -jnp.inf)
        p = jax.nn.softmax(s.reshape(nb, H, GRID_W, kh * GRID_W), axis=-1).reshape(s.shape)
        return jnp.einsum('bhqik,bikhd->bqhd', p, vs.astype(jnp.float32))

    out = lax.map(row_block, jnp.arange(rows))
    return jnp.moveaxis(out, 0, 1).reshape(nb, T, H, hd)


def gla_chunk_scan(q, k, v, log_f):
    nb, T, H, dk = q.shape
    dv = v.shape[-1]
    C = D_CHUNK
    n = T // C

    def chunks(t):
        return t.reshape(nb, n, C, H, t.shape[-1]).transpose(1, 0, 3, 2, 4)

    b = jnp.cumsum(chunks(log_f), axis=3)
    causal = jnp.tril(jnp.ones((C, C), dtype=bool))

    def step(S, inp):
        qc, kc, vc, bc = inp
        blast = bc[:, :, -1:, :]
        inter = jnp.einsum('bhtk,bhkv->bhtv', qc * jnp.exp(bc), S)
        decay = jnp.exp(jnp.where(causal[:, :, None], bc[:, :, :, None, :] - bc[:, :, None, :, :], -jnp.inf))
        att = jnp.einsum('bhtsk,bhsk->bhts', qc[:, :, :, None, :] * decay, kc)
        o = inter + jnp.einsum('bhts,bhsv->bhtv', att, vc)
        S = jnp.exp(blast[:, :, 0, :, None]) * S + jnp.einsum('bhsk,bhsv->bhkv', kc * jnp.exp(blast - bc), vc)
        return S, o

    S0 = jnp.zeros((nb, H, dk, dv), jnp.float32)
    _, o = lax.scan(step, S0, (chunks(q), chunks(k), chunks(v), b))
    return o.transpose(1, 0, 3, 2, 4).reshape(nb, T, H, dv)


def hgrn2_direction(q, i, z, lb, reverse):
    f = (lb + (1.0 - lb) * jax.nn.sigmoid(z.astype(jnp.float32))).reshape(q.shape)
    k = 1.0 - f
    log_f = jnp.log(f)
    if reverse:
        q, k, i, log_f = (jnp.flip(t, axis=1) for t in (q, k, i, log_f))
    o = gla_chunk_scan(q, k, i, log_f)
    return jnp.flip(o, axis=1) if reverse else o


def even_layer(x, w_in, sink, w_out, t5_table):
    nb, T, _ = x.shape
    qa, ka, va, qb, kb, vb, ga, gb = _split(x @ w_in, EVEN_SPLITS)
    t5 = t5_table.astype(jnp.float32)
    W = A_HALF_WINDOW
    rel_a = np.arange(3 * W)[None, :] - W - np.arange(W)[:, None]
    bias_a = jnp.transpose(t5[:, :A_Q_HEADS][t5_buckets(rel_a)], (2, 0, 1)).reshape(A_KV_HEADS, A_GROUP, W, 3 * W)
    oa, _ = banded_attention(qa.reshape(nb, T, A_KV_HEADS, A_GROUP, HEAD_DIM), ka.reshape(nb, T, A_KV_HEADS, HEAD_DIM),
                             va.reshape(nb, T, A_KV_HEADS, HEAD_DIM), bias_a,
                             sink.astype(jnp.float32).reshape(A_KV_HEADS, A_GROUP))
    oa = oa.reshape(nb, T, A_Q_W)
    qb = qb.reshape(nb, T, N_DIL, B_HEADS, HEAD_DIM)
    kb = kb.reshape(nb, T, N_DIL, B_HEADS, HEAD_DIM)
    vb = vb.reshape(nb, T, N_DIL, B_HEADS, HEAD_DIM)
    outs, lses = [], []
    for g, (win, d) in enumerate(B_PAIRS):
        half = win // (2 * d)
        rel = (np.arange(3 * half)[None, :] - half - np.arange(half)[:, None]) * d
        cols = t5[:, A_Q_HEADS + g * B_HEADS: A_Q_HEADS + (g + 1) * B_HEADS]
        bias = jnp.transpose(cols[t5_buckets(rel)], (2, 0, 1))[:, None]
        o, lse = banded_attention(_to_dilated(qb[:, :, g, :, None, :], d), _to_dilated(kb[:, :, g], d),
                                  _to_dilated(vb[:, :, g], d), bias, None)
        outs.append(_from_dilated(o[:, :, :, 0], nb, d))
        lses.append(_from_dilated(lse[:, :, :, 0], nb, d))
    wts = jax.nn.softmax(jnp.stack(lses), axis=0)
    ob = jnp.einsum('gbth,gbthd->bthd', wts, jnp.stack(outs)).reshape(nb, T, B_W)
    y = jnp.concatenate([oa * jax.nn.silu(ga.astype(jnp.float32)), ob * jax.nn.silu(gb.astype(jnp.float32))], axis=-1)
    return y.astype(x.dtype) @ w_out


def odd_layer(x, w_in, rpb, lb, gnorm, w_out):
    nb, T, _ = x.shape
    qc, kc, vc, qd, idd, zf, zb, gc, gd = _split(x @ w_in, ODD_SPLITS)
    oc = neighbourhood_attention(qc.reshape(nb, T, C_HEADS, HEAD_DIM), kc.reshape(nb, T, C_HEADS, HEAD_DIM),
                                 vc.reshape(nb, T, C_HEADS, HEAD_DIM), rpb).reshape(nb, T, C_W)
    qh = qd.astype(jnp.float32).reshape(nb, T, D_HEADS, D_HEAD_DIM)
    ih = idd.astype(jnp.float32).reshape(nb, T, D_HEADS, D_HEAD_DIM)
    od = hgrn2_direction(qh, ih, zf, lb[0], False) + hgrn2_direction(qh, ih, zb, lb[1], True)
    od = od * lax.rsqrt(jnp.mean(jnp.square(od), axis=-1, keepdims=True) + RMS_EPS)
    od = od.reshape(nb, T, D_W) * gnorm.astype(jnp.float32)
    y = jnp.concatenate([oc * jax.nn.silu(gc.astype(jnp.float32)), od * jax.nn.silu(gd.astype(jnp.float32))], axis=-1)
    return y.astype(x.dtype) @ w_out


def trunk(x, t5_table, w_in_even, sink_a, w_out_even, w_in_odd, rpb_c, lb_d, gnorm_d, w_out_odd, ln_g, ln_b):
    lbs = jnp.cumsum(jax.nn.softmax(lb_d.astype(jnp.float32), axis=1), axis=1)
    lbs = lbs - lbs[:, :1]
    for l in range(DEPTH):
        if l % 2 == 0:
            j = l // 2
            y = even_layer(x, w_in_even[j], sink_a[j], w_out_even[j], t5_table)
        else:
            j = l // 2
            y = odd_layer(x, w_in_odd[j], rpb_c[j], lbs[:, l], gnorm_d[j], w_out_odd[j])
        x = layer_norm(ALPHA * x + y, ln_g[l], ln_b[l])
    return x


def setup_inputs(seed: int = 0) -> dict:
    key = jax.random.key(seed)
    ks = jax.random.split(key, 13)
    nrm = jax.random.normal
    return {
        "x_prompt": nrm(ks[0], (BATCH, SEQ, D_MODEL), jnp.float32),
        "x_sample": nrm(ks[1], (DEC_BATCH, DEC_SEQ, D_MODEL), jnp.float32),
        "t5_table": 0.2 * nrm(ks[2], (T5_BUCKETS, T5_HEADS), jnp.float32),
        "w_in_even": nrm(ks[3], (N_EVEN, D_MODEL, EVEN_IN), jnp.float32) * D_MODEL ** -0.5,
        "sink_a": 0.5 * nrm(ks[4], (N_EVEN, A_Q_HEADS), jnp.float32),
        "w_out_even": nrm(ks[5], (N_EVEN, EVEN_OUT, D_MODEL), jnp.float32) * (EVEN_OUT ** -0.5 * BETA),
        "w_in_odd": nrm(ks[6], (N_ODD, D_MODEL, ODD_IN), jnp.float32) * D_MODEL ** -0.5,
        "rpb_c": 0.2 * nrm(ks[7], (N_ODD, C_HEADS, 2 * NA_KH - 1, 2 * NA_KW - 1), jnp.float32),
        "lb_d": 0.5 * nrm(ks[8], (2, DEPTH, D_W), jnp.float32),
        "gnorm_d": 1.0 + 0.01 * nrm(ks[9], (N_ODD, D_W), jnp.float32),
        "w_out_odd": nrm(ks[10], (N_ODD, ODD_OUT, D_MODEL), jnp.float32) * (ODD_OUT ** -0.5 * BETA),
        "ln_g": 1.0 + 0.01 * nrm(ks[11], (DEPTH, D_MODEL), jnp.float32),
        "ln_b": 0.01 * nrm(ks[12], (DEPTH, D_MODEL), jnp.float32),
    }


def reference(x_prompt, x_sample, t5_table, w_in_even, sink_a, w_out_even, w_in_odd, rpb_c, lb_d, gnorm_d, w_out_odd, ln_g, ln_b):
    y_prompt = trunk(x_prompt, t5_table, w_in_even, sink_a, w_out_even, w_in_odd, rpb_c, lb_d, gnorm_d, w_out_odd, ln_g, ln_b)
    y_sample = trunk(x_sample, t5_table, w_in_even, sink_a, w_out_even, w_in_odd, rpb_c, lb_d, gnorm_d, w_out_odd, ln_g, ln_b)
    return (y_prompt, y_sample)
```

```python
import functools
import math

import jax
import jax.numpy as jnp
import numpy as np
from jax import lax
from jax.experimental import pallas as pl
from jax.experimental.pallas import tpu as pltpu

D_MODEL = 1024
DEPTH = 2
HEAD_DIM = 64
A_Q_HEADS = 8
A_KV_HEADS = 2
A_HALF_WINDOW = 128
B_PAIRS = ((128, 1), (512, 4), (2048, 16))
N_DIL = len(B_PAIRS)
B_HEADS = 4
C_HEADS = 8
GRID_W = 64
NA_KH = 8
NA_KW = 16
D_HEADS = 4
D_HEAD_DIM = 128
T5_BUCKETS = 32
T5_MAX_DISTANCE = 1024
ALPHA = (2.0 * DEPTH) ** 0.25
LN_EPS = 1e-5
RMS_EPS = 1e-6

A_Q_W = A_Q_HEADS * HEAD_DIM
A_KV_W = A_KV_HEADS * HEAD_DIM
B_W = B_HEADS * HEAD_DIM
B_QKV_W = N_DIL * B_W
EVEN_IN = 2 * A_Q_W + 2 * A_KV_W + 3 * B_QKV_W + B_W
C_W = C_HEADS * HEAD_DIM
D_W = D_HEADS * D_HEAD_DIM

E_QA = 0
E_KA = E_QA + A_Q_W
E_VA = E_KA + A_KV_W
E_QB = E_VA + A_KV_W
E_KB = E_QB + B_QKV_W
E_VB = E_KB + B_QKV_W
E_GA = E_VB + B_QKV_W
E_GB = E_GA + A_Q_W
O_QC, O_KC, O_VC, O_ID, O_GC, O_GD = (i * C_W for i in range(6))
ODD_BF16_W = 6 * C_W
Z_QD, Z_ZF, Z_ZB = (i * D_W for i in range(3))
ODD_F32_W = 3 * D_W

MASK_VALUE = -1e30
ROW_TILE = 512
ATTN_TILE = 256
SCAN_CHUNK = 128
VMEM_LIMIT_BYTES = 56 * 1024 * 1024

F32 = jnp.float32
BF16 = jnp.bfloat16


def _params(semantics):
    return pltpu.CompilerParams(dimension_semantics=semantics, vmem_limit_bytes=VMEM_LIMIT_BYTES)


def _silu(g):
    return g * (1.0 / (1.0 + jnp.exp(-g)))


def _dot_nt(a, b):
    return lax.dot_general(a, b, (((1,), (1,)), ((), ())), preferred_element_type=F32)


def _proj_body(x_ref, *refs, n_out, col_chunk):
    xb = x_ref[...].astype(BF16)
    for w_ref, o_ref in zip(refs[:n_out], refs[n_out:]):
        n = w_ref.shape[1]
        for c0 in range(0, n, col_chunk):
            c1 = min(c0 + col_chunk, n)
            o_ref[:, c0:c1] = jnp.dot(xb, w_ref[:, c0:c1], preferred_element_type=F32).astype(o_ref.dtype)


def _proj(x2d, ws, dtypes):
    m, k = x2d.shape
    assert m % ROW_TILE == 0
    in_specs = [pl.BlockSpec((ROW_TILE, k), lambda i: (i, 0))]
    in_specs += [pl.BlockSpec(w.shape, lambda i: (0, 0)) for w in ws]
    out_specs = [pl.BlockSpec((ROW_TILE, w.shape[1]), lambda i: (i, 0)) for w in ws]
    out_shape = [jax.ShapeDtypeStruct((m, w.shape[1]), dt) for w, dt in zip(ws, dtypes)]
    return pl.pallas_call(
        functools.partial(_proj_body, n_out=len(ws), col_chunk=512),
        grid=(m // ROW_TILE,),
        in_specs=in_specs,
        out_specs=out_specs,
        out_shape=out_shape,
        compiler_params=_params(("parallel",)),
        name="proj",
    )(x2d, *ws)


def _attn_body(*refs, n_kv, group, has_gate, has_sink, want_lse):
    it = iter(refs)
    q_ref = next(it)
    kl_ref, kc_ref, kr_ref, vl_ref, vc_ref, vr_ref = (next(it) for _ in range(6))
    g_ref = next(it) if has_gate else None
    bias_ref = next(it)
    sink_ref = next(it) if has_sink else None
    o_ref = next(it)
    lse_ref = next(it) if want_lse else None

    q = q_ref[0]
    tq = q.shape[0]
    k = jnp.concatenate([kl_ref[0], kc_ref[0], kr_ref[0]], axis=0)
    v = jnp.concatenate([vl_ref[0], vc_ref[0], vr_ref[0]], axis=0)
    tk = k.shape[0]
    outs, lses = [], []
    for j in range(n_kv):
        heads = range(j * group, (j + 1) * group)
        kj = k[:, j * HEAD_DIM:(j + 1) * HEAD_DIM]
        vj = v[:, j * HEAD_DIM:(j + 1) * HEAD_DIM]
        qs = jnp.concatenate([q[:, h * HEAD_DIM:(h + 1) * HEAD_DIM] for h in heads], axis=0)
        s = _dot_nt(qs, kj) * (HEAD_DIM ** -0.5)
        s = s + bias_ref[0, j * group:(j + 1) * group].reshape(group * tq, tk)
        m = jnp.max(s, axis=-1, keepdims=True)
        if has_sink:
            sk = jnp.concatenate([jnp.full((tq, 1), sink_ref[h], F32) for h in heads], axis=0)
            m = jnp.maximum(m, sk)
        p = jnp.exp(s - m)
        den = jnp.sum(p, axis=-1, keepdims=True)
        if has_sink:
            den = den + jnp.exp(sk - m)
        o = jnp.dot(p.astype(BF16), vj, preferred_element_type=F32) / den
        for gi in range(group):
            outs.append(o[gi * tq:(gi + 1) * tq])
        if want_lse:
            lse = m + jnp.log(den)
            for gi in range(group):
                lses.append(jnp.broadcast_to(lse[gi * tq:(gi + 1) * tq], (tq, HEAD_DIM)))
    out = jnp.concatenate(outs, axis=1)
    if has_gate:
        out = out * _silu(g_ref[0].astype(F32))
    o_ref[0] = out.astype(o_ref.dtype)
    if want_lse:
        lse_ref[0] = jnp.concatenate(lses, axis=1)


def _banded_attention(h, bias, sink, *, seq, dil, halo, n_heads, n_kv, q_off, k_off, v_off, g_off,
                      out_dtype, want_lse):
    nb, _, wtot = h.shape
    length = seq // dil
    tq = min(ATTN_TILE, length)
    assert length % tq == 0 and tq % halo == 0
    nblk = length // tq
    per = tq // halo
    nhalo = length // halo
    qw = n_heads * HEAD_DIM
    kw = n_kv * HEAD_DIM
    hv = h.reshape(nb, length, dil * wtot)

    def col(width, off):
        assert off % width == 0 and (dil == 1 or wtot % width == 0)
        return wtot // width, off // width

    qn, qo = col(qw, q_off)
    kn, ko = col(kw, k_off)
    vn, vo = col(kw, v_off)
    in_specs = [pl.BlockSpec((1, tq, qw), lambda b, r, n: (b, n, r * qn + qo))]
    for cn, co in ((kn, ko), (vn, vo)):
        in_specs += [
            pl.BlockSpec((1, halo, kw), lambda b, r, n, cn=cn, co=co: (b, jnp.maximum(n * per - 1, 0), r * cn + co)),
            pl.BlockSpec((1, tq, kw), lambda b, r, n, cn=cn, co=co: (b, n, r * cn + co)),
            pl.BlockSpec((1, halo, kw), lambda b, r, n, cn=cn, co=co: (b, jnp.minimum((n + 1) * per, nhalo - 1), r * cn + co)),
        ]
    args = [hv] * 7
    if g_off is not None:
        gn, go = col(qw, g_off)
        in_specs.append(pl.BlockSpec((1, tq, qw), lambda b, r, n: (b, n, r * gn + go)))
        args.append(hv)

    def bias_index(b, r, n):
        if nblk == 1:
            return (3, 0, 0, 0)
        return (jnp.where(n == 0, 0, jnp.where(n == nblk - 1, 2, 1)), 0, 0, 0)

    in_specs.append(pl.BlockSpec((1,) + bias.shape[1:], bias_index))
    args.append(bias)
    if sink is not None:
        in_specs.append(pl.BlockSpec(memory_space=pltpu.SMEM))
        args.append(sink)
    out_specs = [pl.BlockSpec((1, tq, qw), lambda b, r, n: (b, n, r))]
    out_shape = [jax.ShapeDtypeStruct((nb, length, dil * qw), out_dtype)]
    if want_lse:
        out_specs.append(pl.BlockSpec((1, tq, qw), lambda b, r, n: (b, n, r)))
        out_shape.append(jax.ShapeDtypeStruct((nb, length, dil * qw), F32))
    outs = pl.pallas_call(
        functools.partial(_attn_body, n_kv=n_kv, group=n_heads // n_kv, has_gate=g_off is not None,
                          has_sink=sink is not None, want_lse=want_lse),
        grid=(nb, dil, nblk),
        in_specs=in_specs,
        out_specs=out_specs,
        out_shape=out_shape,
        compiler_params=_params(("parallel", "parallel", "parallel")),
        name="banded_attention",
    )(*args)
    return [o.reshape(nb, seq, qw) for o in outs]


def _t5_bucket_table(rel):
    nbk = T5_BUCKETS // 2
    ret = (rel > 0).astype(np.int64) * nbk
    n = np.abs(rel)
    max_exact = nbk // 2
    large = max_exact + (np.log(np.maximum(n, 1) / max_exact) / math.log(T5_MAX_DISTANCE / max_exact)
                         * (nbk - max_exact)).astype(np.int64)
    large = np.minimum(large, nbk - 1)
    return ret + np.where(n < max_exact, n, large)


def _band_bias(t5_cols, tq, halo, dil):
    a = np.arange(tq)[:, None]
    c = np.arange(tq + 2 * halo)[None, :]
    rel = c - halo - a
    band = np.abs(rel) <= halo
    left_ok = np.broadcast_to(c >= halo, band.shape)
    right_ok = np.broadcast_to(c < tq + halo, band.shape)
    masks = np.stack([band & left_ok, band, band & right_ok, band & left_ok & right_ok])
    vals = jnp.transpose(t5_cols.astype(F32)[_t5_bucket_table(rel * dil)], (2, 0, 1))
    return jnp.where(masks[:, None], vals[None], MASK_VALUE)


def _neighbourhood_bias(rpb):
    rows_per_tile = NA_KH // 2
    tq = rows_per_tile * GRID_W
    big = 1 << 20
    a = np.arange(tq)
    i, qc = a // GRID_W, a % GRID_W
    c = np.arange(3 * tq)
    w, kc = c // GRID_W, c % GRID_W
    sc = np.clip(qc - NA_KW // 2, 0, GRID_W - NA_KW)
    col_ok = (kc[None, :] >= sc[:, None]) & (kc[None, :] < sc[:, None] + NA_KW)
    dc = np.clip(kc[None, :] - qc[:, None] + NA_KW - 1, 0, 2 * NA_KW - 2)
    masks, drs = [], []
    for r0, rows in ((0, big), (2 * NA_KH, big), (big - rows_per_tile, big), (2 * NA_KH, big)):
        r = r0 + i
        kr = r0 - rows_per_tile + w
        sr = np.clip(r - NA_KH // 2, 0, rows - NA_KH)
        row_ok = (kr[None, :] >= sr[:, None]) & (kr[None, :] < sr[:, None] + NA_KH)
        masks.append(row_ok & col_ok)
        drs.append(np.clip(kr[None, :] - r[:, None] + NA_KH - 1, 0, 2 * NA_KH - 2))
    rpb = rpb.astype(F32)
    tables = [jnp.where(m[None], rpb[:, dr, dc], MASK_VALUE) for m, dr in zip(masks, drs)]
    return jnp.stack(tables)


def _hgrn_body(q_ref, i_ref, z_ref, lb_ref, o_ref, st_ref, *, rev):
    chunk = q_ref.shape[1]

    @pl.when(pl.program_id(1) == 0)
    def _():
        st_ref[...] = jnp.zeros_like(st_ref)

    q = q_ref[0]
    iv = i_ref[0]
    lb = lb_ref[...]
    f = lb + (1.0 - lb) * (1.0 / (1.0 + jnp.exp(-z_ref[0])))
    kk = 1.0 - f
    row = lax.broadcasted_iota(jnp.int32, (chunk, D_W), 0)
    tau = (chunk - 1 - row) if rev else row

    def roll_tau(x, s):
        return pltpu.roll(x, (-s if rev else s) % chunk, axis=0)

    b = jnp.log(f)
    j = 1
    while j < chunk:
        b = b + jnp.where(tau >= j, roll_tau(b, j), 0.0)
        j *= 2

    rt = lax.broadcasted_iota(jnp.int32, (chunk, chunk), 0)
    cs = lax.broadcasted_iota(jnp.int32, (chunk, chunk), 1)
    tau_t = (chunk - 1 - rt) if rev else rt
    tau_s = (chunk - 1 - cs) if rev else cs

    def heads(x):
        return [x[:, h * D_HEAD_DIM:(h + 1) * D_HEAD_DIM] for h in range(D_HEADS)]

    att = [jnp.where(tau_t == tau_s, _dot_nt(qh, kh), 0.0)
           for qh, kh in zip(heads(q.astype(BF16)), heads(kk.astype(BF16)))]
    c = b
    half = 1
    while half < chunk:
        first = (tau & half) == 0
        ref_b = jnp.where(first, c, roll_tau(c, half))
        e = jnp.exp(-jnp.abs(b - ref_b))
        pair = ((tau_t ^ tau_s) < 2 * half) & ((tau_t & half) != 0) & ((tau_s & half) == 0)
        for h, (qh, kh) in enumerate(zip(heads((q * e).astype(BF16)), heads((kk * e).astype(BF16)))):
            att[h] = att[h] + jnp.where(pair, _dot_nt(qh, kh), 0.0)
        c = jnp.where(first, roll_tau(c, -half), c)
        half *= 2

    b_last = b[0:1] if rev else b[chunk - 1:chunk]
    q_in = heads((q * jnp.exp(b)).astype(BF16))
    k_out = heads((kk * jnp.exp(b_last - b)).astype(BF16))
    carry = jnp.exp(b_last)
    ivh = heads(iv)
    for h in range(D_HEADS):
        sl = slice(h * D_HEAD_DIM, (h + 1) * D_HEAD_DIM)
        st = st_ref[h]
        o_ref[0, :, sl] = _dot_nt(q_in[h], st.astype(BF16)) + jnp.dot(att[h].astype(BF16), ivh[h],
                                                                      preferred_element_type=F32)
        upd = lax.dot_general(ivh[h], k_out[h], (((0,), (0,)), ((), ())), preferred_element_type=F32)
        st_ref[h] = st * carry[:, sl] + upd


def _hgrn_scan(hb, zq, lb, *, seq, rev):
    nb = hb.shape[0]
    nchunk = seq // SCAN_CHUNK
    order = (lambda n: nchunk - 1 - n) if rev else (lambda n: n)
    z_blk = (Z_ZB if rev else Z_ZF) // D_W
    return pl.pallas_call(
        functools.partial(_hgrn_body, rev=rev),
        grid=(nb, nchunk),
        in_specs=[
            pl.BlockSpec((1, SCAN_CHUNK, D_W), lambda b, n: (b, order(n), Z_QD // D_W)),
            pl.BlockSpec((1, SCAN_CHUNK, D_W), lambda b, n: (b, order(n), O_ID // D_W)),
            pl.BlockSpec((1, SCAN_CHUNK, D_W), lambda b, n: (b, order(n), z_blk)),
            pl.BlockSpec((1, D_W), lambda b, n: (0, 0)),
        ],
        out_specs=pl.BlockSpec((1, SCAN_CHUNK, D_W), lambda b, n: (b, order(n), 0)),
        out_shape=jax.ShapeDtypeStruct((nb, seq, D_W), F32),
        scratch_shapes=[pltpu.VMEM((D_HEADS, D_HEAD_DIM, D_HEAD_DIM), F32)],
        compiler_params=_params(("parallel", "arbitrary")),
        name="hgrn_scan",
    )(zq, hb, zq, lb)


def _layer_norm_rows(z, g, b):
    mu = jnp.mean(z, axis=-1, keepdims=True)
    zc = z - mu
    var = jnp.mean(zc * zc, axis=-1, keepdims=True)
    return zc * lax.rsqrt(var + LN_EPS) * g + b


def _even_tail_body(x_ref, ya_ref, o0_ref, o1_ref, o2_ref, l0_ref, l1_ref, l2_ref, gb_ref, w_ref,
                    g_ref, b_ref, out_ref):
    l0, l1, l2 = l0_ref[...], l1_ref[...], l2_ref[...]
    m = jnp.maximum(jnp.maximum(l0, l1), l2)
    e0, e1, e2 = jnp.exp(l0 - m), jnp.exp(l1 - m), jnp.exp(l2 - m)
    ob = (e0 * o0_ref[...] + e1 * o1_ref[...] + e2 * o2_ref[...]) / (e0 + e1 + e2)
    yb = (ob * _silu(gb_ref[...].astype(F32))).astype(BF16)
    y = jnp.dot(ya_ref[...], w_ref[:A_Q_W], preferred_element_type=F32)
    y = y + jnp.dot(yb, w_ref[A_Q_W:], preferred_element_type=F32)
    out_ref[...] = _layer_norm_rows(ALPHA * x_ref[...] + y, g_ref[...], b_ref[...])


def _odd_tail_body(x_ref, yc_ref, of_ref, ob_ref, gd_ref, gn_ref, w_ref, g_ref, b_ref, out_ref):
    od = of_ref[...] + ob_ref[...]
    parts = []
    for h in range(D_HEADS):
        oh = od[:, h * D_HEAD_DIM:(h + 1) * D_HEAD_DIM]
        parts.append(oh * lax.rsqrt(jnp.mean(oh * oh, axis=-1, keepdims=True) + RMS_EPS))
    od = jnp.concatenate(parts, axis=1) * gn_ref[...]
    yd = (od * _silu(gd_ref[...].astype(F32))).astype(BF16)
    y = jnp.dot(yc_ref[...], w_ref[:C_W], preferred_element_type=F32)
    y = y + jnp.dot(yd, w_ref[C_W:], preferred_element_type=F32)
    out_ref[...] = _layer_norm_rows(ALPHA * x_ref[...] + y, g_ref[...], b_ref[...])


def _rows(width, blk=0):
    return pl.BlockSpec((ROW_TILE, width), lambda i: (i, blk))


def _whole(shape):
    return pl.BlockSpec(shape, lambda i: (0,) * len(shape))


def _even_tail(x2d, ya, obs, lses, h2d, w_out, ln_g, ln_b):
    m = x2d.shape[0]
    return pl.pallas_call(
        _even_tail_body,
        grid=(m // ROW_TILE,),
        in_specs=[_rows(D_MODEL), _rows(A_Q_W)] + [_rows(B_W)] * 6 + [_rows(B_W, E_GB // B_W)]
        + [_whole(w_out.shape), _whole((1, D_MODEL)), _whole((1, D_MODEL))],
        out_specs=_rows(D_MODEL),
        out_shape=jax.ShapeDtypeStruct((m, D_MODEL), F32),
        compiler_params=_params(("parallel",)),
        name="even_tail",
    )(x2d, ya, *obs, *lses, h2d, w_out, ln_g, ln_b)


def _odd_tail(x2d, yc, o_f, o_b, hb2d, gnorm, w_out, ln_g, ln_b):
    m = x2d.shape[0]
    return pl.pallas_call(
        _odd_tail_body,
        grid=(m // ROW_TILE,),
        in_specs=[_rows(D_MODEL), _rows(C_W), _rows(D_W), _rows(D_W), _rows(D_W, O_GD // D_W),
                  _whole((1, D_W)), _whole(w_out.shape), _whole((1, D_MODEL)), _whole((1, D_MODEL))],
        out_specs=_rows(D_MODEL),
        out_shape=jax.ShapeDtypeStruct((m, D_MODEL), F32),
        compiler_params=_params(("parallel",)),
        name="odd_tail",
    )(x2d, yc, o_f, o_b, hb2d, gnorm, w_out, ln_g, ln_b)


def _even_layer(x, w_in, sink, w_out, t5_table, ln_g, ln_b):
    nb, seq, _ = x.shape
    x2d = x.reshape(nb * seq, D_MODEL)
    (h2d,) = _proj(x2d, [w_in.astype(BF16)], [BF16])
    h = h2d.reshape(nb, seq, EVEN_IN)
    tile_a = min(ATTN_TILE, seq)
    (ya,) = _banded_attention(
        h, _band_bias(t5_table[:, :A_Q_HEADS], tile_a, A_HALF_WINDOW, 1), sink.astype(F32),
        seq=seq, dil=1, halo=A_HALF_WINDOW, n_heads=A_Q_HEADS, n_kv=A_KV_HEADS,
        q_off=E_QA, k_off=E_KA, v_off=E_VA, g_off=E_GA, out_dtype=BF16, want_lse=False)
    obs, lses = [], []
    for g, (win, dil) in enumerate(B_PAIRS):
        halo = win // (2 * dil)
        tile_b = min(ATTN_TILE, seq // dil)
        cols = t5_table[:, A_Q_HEADS + g * B_HEADS:A_Q_HEADS + (g + 1) * B_HEADS]
        o, lse = _banded_attention(
            h, _band_bias(cols, tile_b, halo, dil), None,
            seq=seq, dil=dil, halo=halo, n_heads=B_HEADS, n_kv=B_HEADS,
            q_off=E_QB + g * B_W, k_off=E_KB + g * B_W, v_off=E_VB + g * B_W, g_off=None,
            out_dtype=F32, want_lse=True)
        obs.append(o.reshape(nb * seq, B_W))
        lses.append(lse.reshape(nb * seq, B_W))
    out = _even_tail(x2d, ya.reshape(nb * seq, A_Q_W), obs, lses, h2d, w_out.astype(BF16),
                     ln_g.reshape(1, D_MODEL), ln_b.reshape(1, D_MODEL))
    return out.reshape(nb, seq, D_MODEL)


def _odd_layer(x, w_in, rpb, lb, gnorm, w_out, ln_g, ln_b):
    nb, seq, _ = x.shape
    assert seq // GRID_W >= 2 * NA_KH
    x2d = x.reshape(nb * seq, D_MODEL)
    qc, kc, vc, qd, idd, zf, zb, gc, gd = jnp.split(w_in, [C_W * i for i in range(1, 9)], axis=-1)
    w_bf16 = jnp.concatenate([qc, kc, vc, idd, gc, gd], axis=1).astype(BF16)
    w_f32 = jnp.concatenate([qd, zf, zb], axis=1).astype(BF16)
    hb2d, zq2d = _proj(x2d, [w_bf16, w_f32], [BF16, F32])
    hb = hb2d.reshape(nb, seq, ODD_BF16_W)
    zq = zq2d.reshape(nb, seq, ODD_F32_W)
    (yc,) = _banded_attention(
        hb, _neighbourhood_bias(rpb), None,
        seq=seq, dil=1, halo=(NA_KH // 2) * GRID_W, n_heads=C_HEADS, n_kv=C_HEADS,
        q_off=O_QC, k_off=O_KC, v_off=O_VC, g_off=O_GC, out_dtype=BF16, want_lse=False)
    o_f = _hgrn_scan(hb, zq, lb[0].reshape(1, D_W), seq=seq, rev=False)
    o_b = _hgrn_scan(hb, zq, lb[1].reshape(1, D_W), seq=seq, rev=True)
    out = _odd_tail(x2d, yc.reshape(nb * seq, C_W), o_f.reshape(nb * seq, D_W), o_b.reshape(nb * seq, D_W),
                    hb2d, gnorm.reshape(1, D_W).astype(F32), w_out.astype(BF16),
                    ln_g.reshape(1, D_MODEL), ln_b.reshape(1, D_MODEL))
    return out.reshape(nb, seq, D_MODEL)


def _trunk(x, t5_table, w_in_even, sink_a, w_out_even, w_in_odd, rpb_c, lb_d, gnorm_d, w_out_odd, ln_g, ln_b):
    lbs = jnp.cumsum(jax.nn.softmax(lb_d.astype(F32), axis=1), axis=1)
    lbs = lbs - lbs[:, :1]
    for layer in range(DEPTH):
        j = layer // 2
        if layer % 2 == 0:
            x = _even_layer(x, w_in_even[j], sink_a[j], w_out_even[j], t5_table, ln_g[layer], ln_b[layer])
        else:
            x = _odd_layer(x, w_in_odd[j], rpb_c[j], lbs[:, layer], gnorm_d[j], w_out_odd[j],
                           ln_g[layer], ln_b[layer])
    return x


def kernel(x_prompt, x_sample, t5_table, w_in_even, sink_a, w_out_even, w_in_odd, rpb_c, lb_d, gnorm_d,
           w_out_odd, ln_g, ln_b):
    weights = (t5_table, w_in_even, sink_a, w_out_even, w_in_odd, rpb_c, lb_d, gnorm_d, w_out_odd, ln_g, ln_b)
    return (_trunk(x_prompt, *weights), _trunk(x_sample, *weights))
```

```python
import functools
import math

import jax
import jax.numpy as jnp
import numpy as np
from jax import lax
from jax.experimental import pallas as pl
from jax.experimental.pallas import tpu as pltpu

D_MODEL = 1024
DEPTH = 2
HEAD_DIM = 64
A_Q_HEADS = 8
A_KV_HEADS = 2
A_HALF_WINDOW = 128
B_PAIRS = ((128, 1), (512, 4), (2048, 16))
N_DIL = len(B_PAIRS)
B_HEADS = 4
C_HEADS = 8
GRID_W = 64
NA_KH = 8
NA_KW = 16
D_HEADS = 4
D_HEAD_DIM = 128
T5_BUCKETS = 32
T5_MAX_DISTANCE = 1024
ALPHA = (2.0 * DEPTH) ** 0.25
LN_EPS = 1e-5
RMS_EPS = 1e-6

A_Q_W = A_Q_HEADS * HEAD_DIM
A_KV_W = A_KV_HEADS * HEAD_DIM
B_W = B_HEADS * HEAD_DIM
C_W = C_HEADS * HEAD_DIM
D_W = D_HEADS * D_HEAD_DIM

N_QA = 0
N_KA = N_QA + A_Q_W
N_VA = N_KA + A_KV_W
N_QB = N_VA + A_KV_W
N_KB = N_QB + B_W
N_VB = N_KB + B_W
N_GA = N_VB + B_W
N_GB = N_GA + A_Q_W
EVEN_NAT_W = N_GB + B_W
R_Q, R_K, R_V = 0, B_W, 2 * B_W
EVEN_DIL_W = 3 * B_W
O_QC, O_KC, O_VC, O_ID, O_GC, O_GD = (i * C_W for i in range(6))
ODD_BF16_W = 6 * C_W
Z_QD, Z_ZF, Z_ZB = (i * D_W for i in range(3))
ODD_F32_W = 3 * D_W

LANES = 128
MASK_VALUE = -1e30
ROW_TILE = 512
ATTN_TILE = 256
ATTN_ROWS = 128
SCAN_CHUNK = 128
PROJ_COLS = 768
VMEM_LIMIT_BYTES = 56 * 1024 * 1024

F32 = jnp.float32
BF16 = jnp.bfloat16


def _params(semantics):
    return pltpu.CompilerParams(dimension_semantics=semantics, vmem_limit_bytes=VMEM_LIMIT_BYTES)


def _silu(g):
    return g * (1.0 / (1.0 + jnp.exp(-g)))


def _dot_nt(a, b):
    return lax.dot_general(a, b, (((1,), (1,)), ((), ())), preferred_element_type=F32)


def _proj_body(x_ref, *refs, dils):
    n_out = len(dils)
    tile, width = x_ref.shape[1], x_ref.shape[2]
    lane_tiles = width // LANES
    xs = {}
    if any(dil > 1 for dil in dils):
        xt_ref = refs[2 * n_out]
        for c in range(lane_tiles):
            xt_ref[c] = x_ref[0, :, c * LANES:(c + 1) * LANES]
    for w_ref, o_ref, dil in zip(refs[:n_out], refs[n_out:2 * n_out], dils):
        sub = tile // dil
        if dil not in xs:
            if dil == 1:
                xv = x_ref[0]
            else:
                xv = jnp.concatenate(
                    [jnp.concatenate([xt_ref[c, pl.ds(r, sub, stride=dil), :] for c in range(lane_tiles)], axis=1)
                     for r in range(dil)], axis=0)
            xs[dil] = xv.astype(BF16)
        n = w_ref.shape[1]
        for c0 in range(0, n, PROJ_COLS):
            c1 = min(c0 + PROJ_COLS, n)
            res = jnp.dot(xs[dil], w_ref[:, c0:c1], preferred_element_type=F32).astype(o_ref.dtype)
            if dil == 1:
                o_ref[0, :, c0:c1] = res
            else:
                for r in range(dil):
                    o_ref[0, r, :, c0:c1] = res[r * sub:(r + 1) * sub]


def _proj(x, ws, dtypes, dils):
    nb, seq, k = x.shape
    assert seq % ROW_TILE == 0
    in_specs = [pl.BlockSpec((1, ROW_TILE, k), lambda b, i: (b, i, 0))]
    in_specs += [pl.BlockSpec(w.shape, lambda b, i: (0, 0)) for w in ws]
    out_specs, out_shape = [], []
    for w, dt, dil in zip(ws, dtypes, dils):
        n = w.shape[1]
        if dil == 1:
            out_specs.append(pl.BlockSpec((1, ROW_TILE, n), lambda b, i: (b, i, 0)))
            out_shape.append(jax.ShapeDtypeStruct((nb, seq, n), dt))
        else:
            assert ROW_TILE % (16 * dil) == 0
            out_specs.append(pl.BlockSpec((1, dil, ROW_TILE // dil, n), lambda b, i: (b, 0, i, 0)))
            out_shape.append(jax.ShapeDtypeStruct((nb, dil, seq // dil, n), dt))
    return pl.pallas_call(
        functools.partial(_proj_body, dils=tuple(dils)),
        grid=(nb, seq // ROW_TILE),
        in_specs=in_specs,
        out_specs=out_specs,
        out_shape=out_shape,
        scratch_shapes=[pltpu.VMEM((k // LANES, ROW_TILE, LANES), F32)] if max(dils) > 1 else [],
        compiler_params=_params(("parallel", "parallel")),
        name="proj",
    )(x, *ws)


def _attn_body(*refs, n_heads, n_kv, has_gate, has_sink, want_lse):
    it = iter(refs)
    q_ref = next(it)
    kl_ref, kc_ref, kr_ref, vl_ref, vc_ref, vr_ref = (next(it) for _ in range(6))
    g_ref = next(it) if has_gate else None
    bias_ref = next(it)
    sink_ref = next(it) if has_sink else None
    o_ref = next(it)
    lse_ref = next(it) if want_lse else None

    tq = q_ref.shape[2]
    rows = min(ATTN_ROWS, tq)
    k = jnp.concatenate([kl_ref[0, 0], kc_ref[0, 0], kr_ref[0, 0]], axis=0)
    v = jnp.concatenate([vl_ref[0, 0], vc_ref[0, 0], vr_ref[0, 0]], axis=0)
    tk = k.shape[0]
    group = n_heads // n_kv
    low_k = lax.broadcasted_iota(jnp.int32, (tk, LANES), 1) < HEAD_DIM
    low_q = lax.broadcasted_iota(jnp.int32, (rows, LANES), 1) < HEAD_DIM

    def place(x, src_half, dst_half):
        if src_half != dst_half:
            x = jnp.concatenate([x[:, HEAD_DIM:], x[:, :HEAD_DIM]], axis=1)
        return jnp.where(low_k if dst_half == 0 else ~low_k, x, jnp.zeros_like(x))

    ones = [jnp.where(low_k, 1.0, 0.0).astype(BF16), jnp.where(low_k, 0.0, 1.0).astype(BF16)]
    for pair in range(n_heads // 2):
        cols = slice(pair * LANES, (pair + 1) * LANES)
        kmat, vmat = [], []
        for e in range(2):
            kv_head = (2 * pair + e) // group
            tile = slice((kv_head // 2) * LANES, (kv_head // 2 + 1) * LANES)
            kmat.append(place(k[:, tile], kv_head % 2, e))
            vmat.append(jnp.concatenate([place(v[:, tile], kv_head % 2, e), ones[e]], axis=1))
        for r0 in range(0, tq, rows):
            qp = q_ref[0, 0, r0:r0 + rows, cols]
            acc, mx = None, []
            for e in range(2):
                h = 2 * pair + e
                s = _dot_nt(qp, kmat[e]) + bias_ref[0, h, r0:r0 + rows, :]
                m = jnp.max(s, axis=-1, keepdims=True)
                if has_sink:
                    m = jnp.maximum(m, sink_ref[h])
                t = jnp.dot(jnp.exp(s - m).astype(BF16), vmat[e], preferred_element_type=F32)
                acc = t if acc is None else acc + t
                mx.append(m)
            den = acc[:, LANES:]
            if has_sink:
                den = den + jnp.where(low_q, jnp.exp(sink_ref[2 * pair] - mx[0]),
                                      jnp.exp(sink_ref[2 * pair + 1] - mx[1]))
            out = acc[:, :LANES] / den
            if has_gate:
                out = out * _silu(g_ref[0, 0, r0:r0 + rows, cols].astype(F32))
            o_ref[0, 0, r0:r0 + rows, cols] = out.astype(o_ref.dtype)
            if want_lse:
                lse_ref[0, 0, r0:r0 + rows, cols] = jnp.where(low_q, mx[0], mx[1]) + jnp.log(den)


def _banded_attention(h, bias, sink, *, halo, n_heads, n_kv, q_off, k_off, v_off, g_off, out_dtype, want_lse):
    nb, dil, length, _ = h.shape
    tq = min(ATTN_TILE, length)
    assert length % tq == 0 and tq % halo == 0 and n_heads % 2 == 0
    nblk = length // tq
    per = tq // halo
    nhalo = length // halo
    qw = n_heads * HEAD_DIM
    kw = n_kv * HEAD_DIM
    assert q_off % qw == 0 and k_off % kw == 0 and v_off % kw == 0 and kw % LANES == 0

    in_specs = [pl.BlockSpec((1, 1, tq, qw), lambda b, r, n: (b, r, n, q_off // qw))]
    for off in (k_off, v_off):
        co = off // kw
        in_specs += [
            pl.BlockSpec((1, 1, halo, kw), lambda b, r, n, co=co: (b, r, jnp.maximum(n * per - 1, 0), co)),
            pl.BlockSpec((1, 1, tq, kw), lambda b, r, n, co=co: (b, r, n, co)),
            pl.BlockSpec((1, 1, halo, kw), lambda b, r, n, co=co: (b, r, jnp.minimum((n + 1) * per, nhalo - 1), co)),
        ]
    args = [h] * 7
    if g_off is not None:
        assert g_off % qw == 0
        in_specs.append(pl.BlockSpec((1, 1, tq, qw), lambda b, r, n: (b, r, n, g_off // qw)))
        args.append(h)

    def bias_index(b, r, n):
        if nblk == 1:
            return (3, 0, 0, 0)
        return (jnp.where(n == 0, 0, jnp.where(n == nblk - 1, 2, 1)), 0, 0, 0)

    assert bias.shape == (4, n_heads, tq, tq + 2 * halo)
    in_specs.append(pl.BlockSpec((1,) + bias.shape[1:], bias_index))
    args.append(bias)
    if sink is not None:
        in_specs.append(pl.BlockSpec(memory_space=pltpu.SMEM))
        args.append(sink)
    out_specs = [pl.BlockSpec((1, 1, tq, qw), lambda b, r, n: (b, r, n, 0))]
    out_shape = [jax.ShapeDtypeStruct((nb, dil, length, qw), out_dtype)]
    if want_lse:
        out_specs.append(pl.BlockSpec((1, 1, tq, qw), lambda b, r, n: (b, r, n, 0)))
        out_shape.append(jax.ShapeDtypeStruct((nb, dil, length, qw), F32))
    return pl.pallas_call(
        functools.partial(_attn_body, n_heads=n_heads, n_kv=n_kv, has_gate=g_off is not None,
                          has_sink=sink is not None, want_lse=want_lse),
        grid=(nb, dil, nblk),
        in_specs=in_specs,
        out_specs=out_specs,
        out_shape=out_shape,
        compiler_params=_params(("parallel", "parallel", "parallel")),
        name="banded_attention",
    )(*args)


def _t5_bucket_table(rel):
    nbk = T5_BUCKETS // 2
    ret = (rel > 0).astype(np.int64) * nbk
    n = np.abs(rel)
    max_exact = nbk // 2
    large = max_exact + (np.log(np.maximum(n, 1) / max_exact) / math.log(T5_MAX_DISTANCE / max_exact)
                         * (nbk - max_exact)).astype(np.int64)
    large = np.minimum(large, nbk - 1)
    return ret + np.where(n < max_exact, n, large)


def _lookup(index, table):
    idx = jnp.asarray(index.astype(np.int32))
    out = jnp.zeros((table.shape[1],) + index.shape, F32)
    for i in range(table.shape[0]):
        out = jnp.where(idx == i, table[i].reshape((-1,) + (1,) * index.ndim), out)
    return out


def _band_bias(t5_cols, tq, halo, dil):
    a = np.arange(tq)[:, None]
    c = np.arange(tq + 2 * halo)[None, :]
    rel = c - halo - a
    band = np.abs(rel) <= halo
    left_ok = np.broadcast_to(c >= halo, band.shape)
    right_ok = np.broadcast_to(c < tq + halo, band.shape)
    masks = np.stack([band & left_ok, band, band & right_ok, band & left_ok & right_ok])
    vals = _lookup(_t5_bucket_table(rel * dil), t5_cols.astype(F32))
    return jnp.where(masks[:, None], vals[None], MASK_VALUE)


def _neighbourhood_bias(rpb):
    rows_per_tile = NA_KH // 2
    big = 1 << 20
    qc = np.arange(GRID_W)[:, None]
    kc = np.arange(GRID_W)[None, :]
    sc = np.clip(qc - NA_KW // 2, 0, GRID_W - NA_KW)
    col_ok = (kc >= sc) & (kc < sc + NA_KW)
    dc = np.clip(kc - qc + NA_KW - 1, 0, 2 * NA_KW - 2)
    heads = rpb.shape[0]
    by_dc = _lookup(dc, jnp.transpose(rpb.astype(F32), (2, 0, 1)).reshape(2 * NA_KW - 1, -1))
    by_dc = by_dc.reshape(heads, 2 * NA_KH - 1, GRID_W, GRID_W)
    blocks = jnp.where(col_ok, by_dc, MASK_VALUE)
    masked = jnp.full((heads, GRID_W, GRID_W), MASK_VALUE, F32)
    tables = []
    for r0, rows in ((0, big), (2 * NA_KH, big), (big - rows_per_tile, big), (2 * NA_KH, big)):
        lines = []
        for i in range(rows_per_tile):
            r = r0 + i
            sr = min(max(r - NA_KH // 2, 0), rows - NA_KH)
            line = []
            for w in range(3 * rows_per_tile):
                kr = r0 - rows_per_tile + w
                line.append(blocks[:, kr - r + NA_KH - 1] if sr <= kr < sr + NA_KH else masked)
            lines.append(jnp.concatenate(line, axis=2))
        tables.append(jnp.concatenate(lines, axis=1))
    return jnp.stack(tables)


def _hgrn_body(q_ref, i_ref, z_ref, lb_ref, o_ref, st_ref, *, rev):
    chunk = q_ref.shape[1]

    @pl.when(pl.program_id(1) == 0)
    def _():
        st_ref[...] = jnp.zeros_like(st_ref)

    q = q_ref[0]
    iv = i_ref[0]
    lb = lb_ref[...]
    f = lb + (1.0 - lb) * (1.0 / (1.0 + jnp.exp(-z_ref[0])))
    kk = 1.0 - f
    row = lax.broadcasted_iota(jnp.int32, (chunk, D_W), 0)
    tau = (chunk - 1 - row) if rev else row

    def roll_tau(x, s):
        return pltpu.roll(x, (-s if rev else s) % chunk, axis=0)

    b = jnp.log(f)
    j = 1
    while j < chunk:
        b = b + jnp.where(tau >= j, roll_tau(b, j), 0.0)
        j *= 2

    rt = lax.broadcasted_iota(jnp.int32, (chunk, chunk), 0)
    cs = lax.broadcasted_iota(jnp.int32, (chunk, chunk), 1)
    tau_t = (chunk - 1 - rt) if rev else rt
    tau_s = (chunk - 1 - cs) if rev else cs

    def heads(x):
        return [x[:, h * D_HEAD_DIM:(h + 1) * D_HEAD_DIM] for h in range(D_HEADS)]

    att = [jnp.where(tau_t == tau_s, _dot_nt(qh, kh), 0.0)
           for qh, kh in zip(heads(q.astype(BF16)), heads(kk.astype(BF16)))]
    c = b
    half = 1
    while half < chunk:
        first = (tau & half) == 0
        ref_b = jnp.where(first, c, roll_tau(c, half))
        e = jnp.exp(-jnp.abs(b - ref_b))
        pair = ((tau_t ^ tau_s) < 2 * half) & ((tau_t & half) != 0) & ((tau_s & half) == 0)
        for h, (qh, kh) in enumerate(zip(heads((q * e).astype(BF16)), heads((kk * e).astype(BF16)))):
            att[h] = att[h] + jnp.where(pair, _dot_nt(qh, kh), 0.0)
        c = jnp.where(first, roll_tau(c, -half), c)
        half *= 2

    b_last = b[0:1] if rev else b[chunk - 1:chunk]
    q_in = heads((q * jnp.exp(b)).astype(BF16))
    k_out = heads((kk * jnp.exp(b_last - b)).astype(BF16))
    carry = jnp.exp(b_last)
    ivh = heads(iv)
    for h in range(D_HEADS):
        sl = slice(h * D_HEAD_DIM, (h + 1) * D_HEAD_DIM)
        st = st_ref[h]
        o_ref[0, :, sl] = _dot_nt(q_in[h], st.astype(BF16)) + jnp.dot(att[h].astype(BF16), ivh[h],
                                                                      preferred_element_type=F32)
        upd = lax.dot_general(ivh[h], k_out[h], (((0,), (0,)), ((), ())), preferred_element_type=F32)
        st_ref[h] = st * carry[:, sl] + upd


def _hgrn_scan(hb, zq, lb, *, rev):
    nb, seq, _ = hb.shape
    nchunk = seq // SCAN_CHUNK
    order = (lambda n: nchunk - 1 - n) if rev else (lambda n: n)
    z_blk = (Z_ZB if rev else Z_ZF) // D_W
    return pl.pallas_call(
        functools.partial(_hgrn_body, rev=rev),
        grid=(nb, nchunk),
        in_specs=[
            pl.BlockSpec((1, SCAN_CHUNK, D_W), lambda b, n: (b, order(n), Z_QD // D_W)),
            pl.BlockSpec((1, SCAN_CHUNK, D_W), lambda b, n: (b, order(n), O_ID // D_W)),
            pl.BlockSpec((1, SCAN_CHUNK, D_W), lambda b, n: (b, order(n), z_blk)),
            pl.BlockSpec((1, D_W), lambda b, n: (0, 0)),
        ],
        out_specs=pl.BlockSpec((1, SCAN_CHUNK, D_W), lambda b, n: (b, order(n), 0)),
        out_shape=jax.ShapeDtypeStruct((nb, seq, D_W), F32),
        scratch_shapes=[pltpu.VMEM((D_HEADS, D_HEAD_DIM, D_HEAD_DIM), F32)],
        compiler_params=_params(("parallel", "arbitrary")),
        name="hgrn_scan",
    )(zq, hb, zq, lb)


def _layer_norm_rows(z, g, b):
    mu = jnp.mean(z, axis=-1, keepdims=True)
    zc = z - mu
    var = jnp.mean(zc * zc, axis=-1, keepdims=True)
    return zc * lax.rsqrt(var + LN_EPS) * g + b


def _natural_order(blk_ref, scr_ref):
    dil, sub = blk_ref.shape[1], blk_ref.shape[2]
    lane_tiles = scr_ref.shape[0]
    for r in range(dil):
        for c in range(lane_tiles):
            scr_ref[c, pl.ds(r, sub, stride=dil), :] = blk_ref[0, r, :, c * LANES:(c + 1) * LANES]
    return jnp.concatenate([scr_ref[c] for c in range(lane_tiles)], axis=1)


def _even_tail_body(x_ref, ya_ref, o0_ref, l0_ref, o1_ref, l1_ref, o2_ref, l2_ref, gb_ref, w_ref,
                    g_ref, b_ref, out_ref, so1_ref, sl1_ref, so2_ref, sl2_ref):
    o0, l0 = o0_ref[0], l0_ref[0]
    o1, l1 = _natural_order(o1_ref, so1_ref), _natural_order(l1_ref, sl1_ref)
    o2, l2 = _natural_order(o2_ref, so2_ref), _natural_order(l2_ref, sl2_ref)
    m = jnp.maximum(jnp.maximum(l0, l1), l2)
    e0, e1, e2 = jnp.exp(l0 - m), jnp.exp(l1 - m), jnp.exp(l2 - m)
    ob = (e0 * o0 + e1 * o1 + e2 * o2) / (e0 + e1 + e2)
    yb = (ob * _silu(gb_ref[0].astype(F32))).astype(BF16)
    y = jnp.dot(ya_ref[0], w_ref[:A_Q_W], preferred_element_type=F32)
    y = y + jnp.dot(yb, w_ref[A_Q_W:], preferred_element_type=F32)
    out_ref[0] = _layer_norm_rows(ALPHA * x_ref[0] + y, g_ref[...], b_ref[...])


def _odd_tail_body(x_ref, yc_ref, of_ref, ob_ref, gd_ref, gn_ref, w_ref, g_ref, b_ref, out_ref):
    od = of_ref[0] + ob_ref[0]
    parts = []
    for h in range(D_HEADS):
        oh = od[:, h * D_HEAD_DIM:(h + 1) * D_HEAD_DIM]
        parts.append(oh * lax.rsqrt(jnp.mean(oh * oh, axis=-1, keepdims=True) + RMS_EPS))
    od = jnp.concatenate(parts, axis=1) * gn_ref[...]
    yd = (od * _silu(gd_ref[0].astype(F32))).astype(BF16)
    y = jnp.dot(yc_ref[0], w_ref[:C_W], preferred_element_type=F32)
    y = y + jnp.dot(yd, w_ref[C_W:], preferred_element_type=F32)
    out_ref[0] = _layer_norm_rows(ALPHA * x_ref[0] + y, g_ref[...], b_ref[...])


def _rows(width, blk=0):
    return pl.BlockSpec((1, ROW_TILE, width), lambda b, i: (b, i, blk))


def _residue_rows(dil, width):
    return pl.BlockSpec((1, dil, ROW_TILE // dil, width), lambda b, i: (b, 0, i, 0))


def _whole(shape):
    return pl.BlockSpec(shape, lambda b, i: (0,) * len(shape))


def _even_tail(x, ya, obs, lses, h_nat, w_out, ln_g, ln_b):
    nb, seq, _ = x.shape
    group_specs = []
    for _, dil in B_PAIRS:
        group_specs += [_rows(B_W) if dil == 1 else _residue_rows(dil, B_W)] * 2
    groups = [a for pair in zip(obs, lses) for a in pair]
    return pl.pallas_call(
        _even_tail_body,
        grid=(nb, seq // ROW_TILE),
        in_specs=[_rows(D_MODEL), _rows(A_Q_W)] + group_specs + [_rows(B_W, N_GB // B_W)]
        + [_whole(w_out.shape), _whole((1, D_MODEL)), _whole((1, D_MODEL))],
        out_specs=_rows(D_MODEL),
        out_shape=jax.ShapeDtypeStruct((nb, seq, D_MODEL), F32),
        scratch_shapes=[pltpu.VMEM((B_W // LANES, ROW_TILE, LANES), F32)] * 4,
        compiler_params=_params(("parallel", "parallel")),
        name="even_tail",
    )(x, ya, *groups, h_nat, w_out, ln_g, ln_b)


def _odd_tail(x, yc, o_f, o_b, hb, gnorm, w_out, ln_g, ln_b):
    nb, seq, _ = x.shape
    return pl.pallas_call(
        _odd_tail_body,
        grid=(nb, seq // ROW_TILE),
        in_specs=[_rows(D_MODEL), _rows(C_W), _rows(D_W), _rows(D_W), _rows(D_W, O_GD // D_W),
                  _whole((1, D_W)), _whole(w_out.shape), _whole((1, D_MODEL)), _whole((1, D_MODEL))],
        out_specs=_rows(D_MODEL),
        out_shape=jax.ShapeDtypeStruct((nb, seq, D_MODEL), F32),
        compiler_params=_params(("parallel", "parallel")),
        name="odd_tail",
    )(x, yc, o_f, o_b, hb, gnorm, w_out, ln_g, ln_b)


def _even_weights(w_in, w_out, sink, t5_table, tiles):
    scale = HEAD_DIM ** -0.5
    qa, ka, va, qb, kb, vb, ga, gb = jnp.split(
        w_in, np.cumsum([A_Q_W, A_KV_W, A_KV_W, N_DIL * B_W, N_DIL * B_W, N_DIL * B_W, A_Q_W])[:7].tolist(), axis=-1)
    grp = lambda w, g: w[:, g * B_W:(g + 1) * B_W]
    w_nat = jnp.concatenate([qa * scale, ka, va, grp(qb, 0) * scale, grp(kb, 0), grp(vb, 0), ga, gb], axis=1)
    w_dil = [jnp.concatenate([grp(qb, g) * scale, grp(kb, g), grp(vb, g)], axis=1) for g in range(1, N_DIL)]
    bias_a = _band_bias(t5_table[:, :A_Q_HEADS], tiles["a"], A_HALF_WINDOW, 1)
    bias_b = []
    for g, (win, dil) in enumerate(B_PAIRS):
        cols = t5_table[:, A_Q_HEADS + g * B_HEADS:A_Q_HEADS + (g + 1) * B_HEADS]
        bias_b.append(_band_bias(cols, tiles["b"][g], win // (2 * dil), dil))
    return dict(w_nat=w_nat.astype(BF16), w_dil=[w.astype(BF16) for w in w_dil], w_out=w_out.astype(BF16),
                sink=sink.astype(F32), bias_a=bias_a, bias_b=bias_b)


def _odd_weights(w_in, w_out, rpb, gnorm):
    scale = HEAD_DIM ** -0.5
    qc, kc, vc, qd, idd, zf, zb, gc, gd = jnp.split(w_in, [C_W * i for i in range(1, 9)], axis=-1)
    return dict(w_bf16=jnp.concatenate([qc * scale, kc, vc, idd, gc, gd], axis=1).astype(BF16),
                w_f32=jnp.concatenate([qd, zf, zb], axis=1).astype(BF16),
                w_out=w_out.astype(BF16), bias_c=_neighbourhood_bias(rpb),
                gnorm=gnorm.reshape(1, D_W).astype(F32))


def _even_layer(x, p, ln_g, ln_b):
    nb, seq, _ = x.shape
    dils = [dil for _, dil in B_PAIRS]
    slabs = _proj(x, [p["w_nat"]] + p["w_dil"], [BF16] * N_DIL, dils)
    h_nat = slabs[0]
    (ya,) = _banded_attention(
        h_nat.reshape(nb, 1, seq, EVEN_NAT_W), p["bias_a"], p["sink"], halo=A_HALF_WINDOW,
        n_heads=A_Q_HEADS, n_kv=A_KV_HEADS, q_off=N_QA, k_off=N_KA, v_off=N_VA, g_off=N_GA,
        out_dtype=BF16, want_lse=False)
    obs, lses = [], []
    for g, (win, dil) in enumerate(B_PAIRS):
        slab = h_nat.reshape(nb, 1, seq, EVEN_NAT_W) if g == 0 else slabs[g]
        offs = (N_QB, N_KB, N_VB) if g == 0 else (R_Q, R_K, R_V)
        o, lse = _banded_attention(
            slab, p["bias_b"][g], None, halo=win // (2 * dil), n_heads=B_HEADS, n_kv=B_HEADS,
            q_off=offs[0], k_off=offs[1], v_off=offs[2], g_off=None, out_dtype=F32, want_lse=True)
        obs.append(o.reshape(nb, seq, B_W) if dil == 1 else o)
        lses.append(lse.reshape(nb, seq, B_W) if dil == 1 else lse)
    return _even_tail(x, ya.reshape(nb, seq, A_Q_W), obs, lses, h_nat, p["w_out"],
                      ln_g.reshape(1, D_MODEL), ln_b.reshape(1, D_MODEL))


def _odd_layer(x, p, lb, ln_g, ln_b):
    nb, seq, _ = x.shape
    assert seq // GRID_W >= 2 * NA_KH
    hb, zq = _proj(x, [p["w_bf16"], p["w_f32"]], [BF16, F32], [1, 1])
    (yc,) = _banded_attention(
        hb.reshape(nb, 1, seq, ODD_BF16_W), p["bias_c"], None, halo=(NA_KH // 2) * GRID_W,
        n_heads=C_HEADS, n_kv=C_HEADS, q_off=O_QC, k_off=O_KC, v_off=O_VC, g_off=O_GC,
        out_dtype=BF16, want_lse=False)
    o_f = _hgrn_scan(hb, zq, lb[0].reshape(1, D_W), rev=False)
    o_b = _hgrn_scan(hb, zq, lb[1].reshape(1, D_W), rev=True)
    return _odd_tail(x, yc.reshape(nb, seq, C_W), o_f, o_b, hb, p["gnorm"], p["w_out"],
                     ln_g.reshape(1, D_MODEL), ln_b.reshape(1, D_MODEL))


def _attention_tiles(seq):
    return dict(a=min(ATTN_TILE, seq), b=[min(ATTN_TILE, seq // dil) for _, dil in B_PAIRS])


def _trunk(x, even, odd, lbs, ln_g, ln_b):
    for layer in range(DEPTH):
        j = layer // 2
        if layer % 2 == 0:
            x = _even_layer(x, even[j], ln_g[layer], ln_b[layer])
        else:
            x = _odd_layer(x, odd[j], lbs[:, layer], ln_g[layer], ln_b[layer])
    return x


def kernel(x_prompt, x_sample, t5_table, w_in_even, sink_a, w_out_even, w_in_odd, rpb_c, lb_d, gnorm_d,
           w_out_odd, ln_g, ln_b):
    lbs = jnp.cumsum(jax.nn.softmax(lb_d.astype(F32), axis=1), axis=1)
    lbs = lbs - lbs[:, :1]
    tiles = _attention_tiles(x_prompt.shape[1])
    assert tiles == _attention_tiles(x_sample.shape[1])
    even = [_even_weights(w_in_even[j], w_out_even[j], sink_a[j], t5_table, tiles) for j in range((DEPTH + 1) // 2)]
    odd = [_odd_weights(w_in_odd[j], w_out_odd[j], rpb_c[j], gnorm_d[j]) for j in range(DEPTH // 2)]
    return tuple(_trunk(x, even, odd, lbs, ln_g, ln_b) for x in (x_prompt, x_sample))
```

```python
import functools
import math

import jax
import jax.numpy as jnp
import numpy as np
from jax import lax
from jax.experimental import pallas as pl
from jax.experimental.pallas import tpu as pltpu

D_MODEL = 1024
DEPTH = 2
HEAD_DIM = 64
A_Q_HEADS = 8
A_KV_HEADS = 2
A_HALF_WINDOW = 128
B_PAIRS = ((128, 1), (512, 4), (2048, 16))
N_DIL = len(B_PAIRS)
B_HEADS = 4
C_HEADS = 8
GRID_W = 64
NA_KH = 8
NA_KW = 16
D_HEADS = 4
D_HEAD_DIM = 128
T5_BUCKETS = 32
T5_MAX_DISTANCE = 1024
ALPHA = (2.0 * DEPTH) ** 0.25
LN_EPS = 1e-5
RMS_EPS = 1e-6

A_Q_W = A_Q_HEADS * HEAD_DIM
A_KV_W = A_KV_HEADS * HEAD_DIM
B_W = B_HEADS * HEAD_DIM
C_W = C_HEADS * HEAD_DIM
D_W = D_HEADS * D_HEAD_DIM

N_QA = 0
N_KA = N_QA + A_Q_W
N_VA = N_KA + A_KV_W
N_QB = N_VA + A_KV_W
N_KB = N_QB + B_W
N_VB = N_KB + B_W
N_GA = N_VB + B_W
N_GB = N_GA + A_Q_W
EVEN_NAT_W = N_GB + B_W
R_Q, R_K, R_V = 0, B_W, 2 * B_W
EVEN_DIL_W = 3 * B_W
O_QC, O_KC, O_VC, O_ID, O_GC, O_GD = (i * C_W for i in range(6))
ODD_BF16_W = 6 * C_W
Z_QD, Z_ZF, Z_ZB = (i * D_W for i in range(3))
ODD_F32_W = 3 * D_W

LANES = 128
MASK_VALUE = -1e30
ROW_TILE = 512
ATTN_TILE = 256
ATTN_SCORES = 128 * 768
SCAN_CHUNK = 128
PROJ_COLS = 768
VMEM_LIMIT_BYTES = 56 * 1024 * 1024

F32 = jnp.float32
BF16 = jnp.bfloat16


def _params(semantics):
    return pltpu.CompilerParams(dimension_semantics=semantics, vmem_limit_bytes=VMEM_LIMIT_BYTES)


def _silu(g):
    return g * (1.0 / (1.0 + jnp.exp(-g)))


def _dot_nt(a, b):
    return lax.dot_general(a, b, (((1,), (1,)), ((), ())), preferred_element_type=F32)


def _proj_body(x_ref, *refs, dils):
    n_out = len(dils)
    tile, width = x_ref.shape[1], x_ref.shape[2]
    lane_tiles = width // LANES
    xs = {}
    if any(dil > 1 for dil in dils):
        xt_ref = refs[2 * n_out]
        for c in range(lane_tiles):
            xt_ref[c] = x_ref[0, :, c * LANES:(c + 1) * LANES]
    for w_ref, o_ref, dil in zip(refs[:n_out], refs[n_out:2 * n_out], dils):
        sub = tile // dil
        if dil not in xs:
            if dil == 1:
                xv = x_ref[0]
            else:
                xv = jnp.concatenate(
                    [jnp.concatenate([xt_ref[c, pl.ds(r, sub, stride=dil), :] for c in range(lane_tiles)], axis=1)
                     for r in range(dil)], axis=0)
            xs[dil] = xv.astype(BF16)
        n = w_ref.shape[1]
        for c0 in range(0, n, PROJ_COLS):
            c1 = min(c0 + PROJ_COLS, n)
            res = jnp.dot(xs[dil], w_ref[:, c0:c1], preferred_element_type=F32).astype(o_ref.dtype)
            if dil == 1:
                o_ref[0, :, c0:c1] = res
            else:
                for r in range(dil):
                    o_ref[0, r, :, c0:c1] = res[r * sub:(r + 1) * sub]


def _proj(x, ws, dtypes, dils):
    nb, seq, k = x.shape
    assert seq % ROW_TILE == 0
    in_specs = [pl.BlockSpec((1, ROW_TILE, k), lambda b, i: (b, i, 0))]
    in_specs += [pl.BlockSpec(w.shape, lambda b, i: (0, 0)) for w in ws]
    out_specs, out_shape = [], []
    for w, dt, dil in zip(ws, dtypes, dils):
        n = w.shape[1]
        if dil == 1:
            out_specs.append(pl.BlockSpec((1, ROW_TILE, n), lambda b, i: (b, i, 0)))
            out_shape.append(jax.ShapeDtypeStruct((nb, seq, n), dt))
        else:
            assert ROW_TILE % (16 * dil) == 0
            out_specs.append(pl.BlockSpec((1, dil, ROW_TILE // dil, n), lambda b, i: (b, 0, i, 0)))
            out_shape.append(jax.ShapeDtypeStruct((nb, dil, seq // dil, n), dt))
    return pl.pallas_call(
        functools.partial(_proj_body, dils=tuple(dils)),
        grid=(nb, seq // ROW_TILE),
        in_specs=in_specs,
        out_specs=out_specs,
        out_shape=out_shape,
        scratch_shapes=[pltpu.VMEM((k // LANES, ROW_TILE, LANES), F32)] if max(dils) > 1 else [],
        compiler_params=_params(("parallel", "parallel")),
        name="proj",
    )(x, *ws)


def _attn_body(*refs, n_heads, n_kv, has_gate, has_sink, want_lse):
    it = iter(refs)
    q_ref = next(it)
    kl_ref, kc_ref, kr_ref, vl_ref, vc_ref, vr_ref = (next(it) for _ in range(6))
    g_ref = next(it) if has_gate else None
    bias_ref = next(it)
    sink_ref = next(it) if has_sink else None
    o_ref = next(it)
    lse_ref = next(it) if want_lse else None

    tq = q_ref.shape[2]
    k = jnp.concatenate([kl_ref[0, 0], kc_ref[0, 0], kr_ref[0, 0]], axis=0)
    v = jnp.concatenate([vl_ref[0, 0], vc_ref[0, 0], vr_ref[0, 0]], axis=0)
    tk = k.shape[0]
    rows = tq
    while rows * tk > ATTN_SCORES and rows % 16 == 0:
        rows //= 2
    group = n_heads // n_kv
    low_k = lax.broadcasted_iota(jnp.int32, (tk, LANES), 1) < HEAD_DIM
    low_q = lax.broadcasted_iota(jnp.int32, (rows, LANES), 1) < HEAD_DIM

    def place(x, src_half, dst_half):
        if src_half != dst_half:
            x = jnp.concatenate([x[:, HEAD_DIM:], x[:, :HEAD_DIM]], axis=1)
        return jnp.where(low_k if dst_half == 0 else ~low_k, x, jnp.zeros_like(x))

    ones = [jnp.where(low_k, 1.0, 0.0).astype(BF16), jnp.where(low_k, 0.0, 1.0).astype(BF16)]
    for pair in range(n_heads // 2):
        cols = slice(pair * LANES, (pair + 1) * LANES)
        kmat, vmat = [], []
        for e in range(2):
            kv_head = (2 * pair + e) // group
            tile = slice((kv_head // 2) * LANES, (kv_head // 2 + 1) * LANES)
            kmat.append(place(k[:, tile], kv_head % 2, e))
            vmat.append(jnp.concatenate([place(v[:, tile], kv_head % 2, e), ones[e]], axis=1))
        for r0 in range(0, tq, rows):
            qp = q_ref[0, 0, r0:r0 + rows, cols]
            acc, mx = None, []
            for e in range(2):
                h = 2 * pair + e
                s = _dot_nt(qp, kmat[e]) + bias_ref[0, h, r0:r0 + rows, :]
                m = jnp.max(s, axis=-1, keepdims=True)
                if has_sink:
                    m = jnp.maximum(m, sink_ref[h])
                t = jnp.dot(jnp.exp(s - m).astype(BF16), vmat[e], preferred_element_type=F32)
                acc = t if acc is None else acc + t
                mx.append(m)
            den = acc[:, LANES:]
            if has_sink:
                den = den + jnp.where(low_q, jnp.exp(sink_ref[2 * pair] - mx[0]),
                                      jnp.exp(sink_ref[2 * pair + 1] - mx[1]))
            out = acc[:, :LANES] / den
            if has_gate:
                out = out * _silu(g_ref[0, 0, r0:r0 + rows, cols].astype(F32))
            o_ref[0, 0, r0:r0 + rows, cols] = out.astype(o_ref.dtype)
            if want_lse:
                lse_ref[0, 0, r0:r0 + rows, cols] = jnp.where(low_q, mx[0], mx[1]) + jnp.log(den)


def _banded_attention(h, bias, sink, *, halo, n_heads, n_kv, q_off, k_off, v_off, g_off, out_dtype, want_lse):
    nb, dil, length, _ = h.shape
    tq = min(ATTN_TILE, length)
    assert length % tq == 0 and tq % halo == 0 and n_heads % 2 == 0
    nblk = length // tq
    per = tq // halo
    nhalo = length // halo
    qw = n_heads * HEAD_DIM
    kw = n_kv * HEAD_DIM
    assert q_off % qw == 0 and k_off % kw == 0 and v_off % kw == 0 and kw % LANES == 0

    in_specs = [pl.BlockSpec((1, 1, tq, qw), lambda b, r, n: (b, r, n, q_off // qw))]
    for off in (k_off, v_off):
        co = off // kw
        in_specs += [
            pl.BlockSpec((1, 1, halo, kw), lambda b, r, n, co=co: (b, r, jnp.maximum(n * per - 1, 0), co)),
            pl.BlockSpec((1, 1, tq, kw), lambda b, r, n, co=co: (b, r, n, co)),
            pl.BlockSpec((1, 1, halo, kw), lambda b, r, n, co=co: (b, r, jnp.minimum((n + 1) * per, nhalo - 1), co)),
        ]
    args = [h] * 7
    if g_off is not None:
        assert g_off % qw == 0
        in_specs.append(pl.BlockSpec((1, 1, tq, qw), lambda b, r, n: (b, r, n, g_off // qw)))
        args.append(h)

    def bias_index(b, r, n):
        if nblk == 1:
            return (3, 0, 0, 0)
        return (jnp.where(n == 0, 0, jnp.where(n == nblk - 1, 2, 1)), 0, 0, 0)

    assert bias.shape == (4, n_heads, tq, tq + 2 * halo)
    in_specs.append(pl.BlockSpec((1,) + bias.shape[1:], bias_index))
    args.append(bias)
    if sink is not None:
        in_specs.append(pl.BlockSpec(memory_space=pltpu.SMEM))
        args.append(sink)
    out_specs = [pl.BlockSpec((1, 1, tq, qw), lambda b, r, n: (b, r, n, 0))]
    out_shape = [jax.ShapeDtypeStruct((nb, dil, length, qw), out_dtype)]
    if want_lse:
        out_specs.append(pl.BlockSpec((1, 1, tq, qw), lambda b, r, n: (b, r, n, 0)))
        out_shape.append(jax.ShapeDtypeStruct((nb, dil, length, qw), F32))
    return pl.pallas_call(
        functools.partial(_attn_body, n_heads=n_heads, n_kv=n_kv, has_gate=g_off is not None,
                          has_sink=sink is not None, want_lse=want_lse),
        grid=(nb, dil, nblk),
        in_specs=in_specs,
        out_specs=out_specs,
        out_shape=out_shape,
        compiler_params=_params(("parallel", "parallel", "parallel")),
        name="banded_attention",
    )(*args)


def _t5_bucket_table(rel):
    nbk = T5_BUCKETS // 2
    ret = (rel > 0).astype(np.int64) * nbk
    n = np.abs(rel)
    max_exact = nbk // 2
    large = max_exact + (np.log(np.maximum(n, 1) / max_exact) / math.log(T5_MAX_DISTANCE / max_exact)
                         * (nbk - max_exact)).astype(np.int64)
    large = np.minimum(large, nbk - 1)
    return ret + np.where(n < max_exact, n, large)


def _lookup(index, table):
    idx = jnp.asarray(index.astype(np.int32))
    out = jnp.zeros((table.shape[1],) + index.shape, F32)
    for i in range(table.shape[0]):
        out = jnp.where(idx == i, table[i].reshape((-1,) + (1,) * index.ndim), out)
    return out


def _band_bias(t5_cols, tq, halo, dil):
    a = np.arange(tq)[:, None]
    c = np.arange(tq + 2 * halo)[None, :]
    rel = c - halo - a
    band = np.abs(rel) <= halo
    left_ok = np.broadcast_to(c >= halo, band.shape)
    right_ok = np.broadcast_to(c < tq + halo, band.shape)
    masks = np.stack([band & left_ok, band, band & right_ok, band & left_ok & right_ok])
    vals = _lookup(_t5_bucket_table(rel * dil), t5_cols.astype(F32))
    return jnp.where(masks[:, None], vals[None], MASK_VALUE)


def _neighbourhood_bias(rpb):
    rows_per_tile = NA_KH // 2
    big = 1 << 20
    qc = np.arange(GRID_W)[:, None]
    kc = np.arange(GRID_W)[None, :]
    sc = np.clip(qc - NA_KW // 2, 0, GRID_W - NA_KW)
    col_ok = (kc >= sc) & (kc < sc + NA_KW)
    dc = np.clip(kc - qc + NA_KW - 1, 0, 2 * NA_KW - 2)
    heads = rpb.shape[0]
    by_dc = _lookup(dc, jnp.transpose(rpb.astype(F32), (2, 0, 1)).reshape(2 * NA_KW - 1, -1))
    by_dc = by_dc.reshape(heads, 2 * NA_KH - 1, GRID_W, GRID_W)
    blocks = jnp.where(col_ok, by_dc, MASK_VALUE)
    masked = jnp.full((heads, GRID_W, GRID_W), MASK_VALUE, F32)
    tables = []
    for r0, rows in ((0, big), (2 * NA_KH, big), (big - rows_per_tile, big), (2 * NA_KH, big)):
        lines = []
        for i in range(rows_per_tile):
            r = r0 + i
            sr = min(max(r - NA_KH // 2, 0), rows - NA_KH)
            line = []
            for w in range(3 * rows_per_tile):
                kr = r0 - rows_per_tile + w
                line.append(blocks[:, kr - r + NA_KH - 1] if sr <= kr < sr + NA_KH else masked)
            lines.append(jnp.concatenate(line, axis=2))
        tables.append(jnp.concatenate(lines, axis=1))
    return jnp.stack(tables)


def _scan_levels():
    return [1 << i for i in range(SCAN_CHUNK.bit_length() - 1)]


def _scan_constants():
    idx = np.arange(SCAN_CHUNK)
    tris, masks = [], []
    for rev in (False, True):
        tau = (SCAN_CHUNK - 1 - idx) if rev else idx
        tau_t, tau_s = tau[:, None], tau[None, :]
        tris.append(np.tile((tau_s <= tau_t).astype(np.float32), (1, 3)))
        own = [tau_t == tau_s]
        for half in _scan_levels():
            own.append(((tau_t ^ tau_s) < 2 * half) & ((tau_t & half) != 0) & ((tau_s & half) == 0))
        masks.append(np.stack(own).astype(np.float32))
    return jnp.asarray(np.stack(tris), BF16), jnp.asarray(np.stack(masks), F32)


ROLL_LEVELS = 2


def _scan_direction(q_ref, i_ref, z_ref, lb, tri_ref, mask_ref, o_ref, st_ref, b_ref, *, rev):
    chunk = SCAN_CHUNK
    q = q_ref[0]
    f = lb + (1.0 - lb) * (1.0 / (1.0 + jnp.exp(-z_ref[0])))
    kk = 1.0 - f

    def heads(x):
        return [x[:, h * D_HEAD_DIM:(h + 1) * D_HEAD_DIM] for h in range(D_HEADS)]

    g = jnp.log(f) * math.log2(math.e)
    g_hi = g.astype(BF16)
    rest = g - g_hi.astype(F32)
    g_mid = rest.astype(BF16)
    g_lo = (rest - g_mid.astype(F32)).astype(BF16)
    b = jnp.dot(tri_ref[...], jnp.concatenate([g_hi, g_mid, g_lo], axis=0), preferred_element_type=F32)
    b_ref[...] = b

    q_bf = q.astype(BF16)
    k_bf = kk.astype(BF16)
    att = [mask_ref[0] * _dot_nt(qh, kh) for qh, kh in zip(heads(q_bf), heads(k_bf))]

    row = lax.broadcasted_iota(jnp.int32, (chunk, D_W), 0)
    tau = (chunk - 1 - row) if rev else row

    def roll_tau(x, s):
        return pltpu.roll(x, (-s if rev else s) % chunk, axis=0)

    c = b
    for level, half in enumerate(_scan_levels()):
        if level < ROLL_LEVELS:
            first = (tau & half) == 0
            ref_b = jnp.where(first, c, roll_tau(c, half))
            expo = -jnp.abs(b - ref_b)
            if level + 1 < ROLL_LEVELS:
                c = jnp.where(first, roll_tau(c, -half), c)
        else:
            pieces = []
            for r0 in range(0, chunk, 2 * half):
                ref_row = (r0 + half) if rev else (r0 + half - 1)
                ref_b = b_ref[ref_row:ref_row + 1, :]
                blk = b[r0:r0 + 2 * half]
                if half % 8 == 0:
                    top, bot = blk[:half], blk[half:]
                    pieces += [top - ref_b, ref_b - bot] if rev else [ref_b - top, bot - ref_b]
                else:
                    pieces.append(-jnp.abs(blk - ref_b))
            expo = jnp.concatenate(pieces, axis=0)
        e = jnp.exp2(expo).astype(BF16)
        for h, (qh, kh) in enumerate(zip(heads(q_bf * e), heads(k_bf * e))):
            att[h] = att[h] + mask_ref[1 + level] * _dot_nt(qh, kh)
        yield

    b_last = b[0:1] if rev else b[chunk - 1:chunk]
    q_in = heads(q_bf * jnp.exp2(b).astype(BF16))
    k_out = heads(k_bf * jnp.exp2(b_last - b).astype(BF16))
    carry = jnp.exp2(b_last)
    ivh = heads(i_ref[0])
    for h in range(D_HEADS):
        sl = slice(h * D_HEAD_DIM, (h + 1) * D_HEAD_DIM)
        st = st_ref[h]
        o_ref[0, :, sl] = _dot_nt(q_in[h], st.astype(BF16)) + jnp.dot(att[h].astype(BF16), ivh[h],
                                                                      preferred_element_type=F32)
        upd = lax.dot_general(ivh[h], k_out[h], (((0,), (0,)), ((), ())), preferred_element_type=F32)
        st_ref[h] = st * carry[:, sl] + upd


def _hgrn_body(qf_ref, qb_ref, if_ref, ib_ref, zf_ref, zb_ref, lb_ref, tri_ref, mask_ref,
               of_ref, ob_ref, st_ref, b_ref):
    @pl.when(pl.program_id(1) == 0)
    def _():
        st_ref[...] = jnp.zeros_like(st_ref)

    scans = [
        _scan_direction(qf_ref, if_ref, zf_ref, lb_ref[0:1], tri_ref.at[0], mask_ref.at[0], of_ref,
                        st_ref.at[0], b_ref.at[0], rev=False),
        _scan_direction(qb_ref, ib_ref, zb_ref, lb_ref[1:2], tri_ref.at[1], mask_ref.at[1], ob_ref,
                        st_ref.at[1], b_ref.at[1], rev=True),
    ]
    while scans:
        scans = [s for s in scans if next(s, True) is None]


def _hgrn_scan(hb, zq, lb):
    nb, seq, _ = hb.shape
    nchunk = seq // SCAN_CHUNK
    tri, masks = _scan_constants()

    def chunk_spec(col, rev):
        if rev:
            return pl.BlockSpec((1, SCAN_CHUNK, D_W), lambda b, n: (b, nchunk - 1 - n, col))
        return pl.BlockSpec((1, SCAN_CHUNK, D_W), lambda b, n: (b, n, col))

    def whole(a):
        return pl.BlockSpec(a.shape, lambda b, n: (0,) * a.ndim)

    return pl.pallas_call(
        _hgrn_body,
        grid=(nb, nchunk),
        in_specs=[chunk_spec(Z_QD // D_W, False), chunk_spec(Z_QD // D_W, True),
                  chunk_spec(O_ID // D_W, False), chunk_spec(O_ID // D_W, True),
                  chunk_spec(Z_ZF // D_W, False), chunk_spec(Z_ZB // D_W, True),
                  whole(lb), whole(tri), whole(masks)],
        out_specs=[chunk_spec(0, False), chunk_spec(0, True)],
        out_shape=[jax.ShapeDtypeStruct((nb, seq, D_W), F32)] * 2,
        scratch_shapes=[pltpu.VMEM((2, D_HEADS, D_HEAD_DIM, D_HEAD_DIM), F32),
                        pltpu.VMEM((2, SCAN_CHUNK, D_W), F32)],
        compiler_params=_params(("parallel", "arbitrary")),
        name="hgrn_scan",
    )(zq, zq, hb, hb, zq, zq, lb, tri, masks)


def _layer_norm_rows(z, g, b):
    mu = jnp.mean(z, axis=-1, keepdims=True)
    zc = z - mu
    var = jnp.mean(zc * zc, axis=-1, keepdims=True)
    return zc * lax.rsqrt(var + LN_EPS) * g + b


def _natural_order(blk_ref, scr_ref):
    dil, sub = blk_ref.shape[1], blk_ref.shape[2]
    lane_tiles = scr_ref.shape[0]
    for r in range(dil):
        for c in range(lane_tiles):
            scr_ref[c, pl.ds(r, sub, stride=dil), :] = blk_ref[0, r, :, c * LANES:(c + 1) * LANES]
    return jnp.concatenate([scr_ref[c] for c in range(lane_tiles)], axis=1)


def _even_tail_body(x_ref, ya_ref, o0_ref, l0_ref, o1_ref, l1_ref, o2_ref, l2_ref, gb_ref, w_ref,
                    g_ref, b_ref, out_ref, so1_ref, sl1_ref, so2_ref, sl2_ref):
    o0, l0 = o0_ref[0], l0_ref[0]
    o1, l1 = _natural_order(o1_ref, so1_ref), _natural_order(l1_ref, sl1_ref)
    o2, l2 = _natural_order(o2_ref, so2_ref), _natural_order(l2_ref, sl2_ref)
    m = jnp.maximum(jnp.maximum(l0, l1), l2)
    e0, e1, e2 = jnp.exp(l0 - m), jnp.exp(l1 - m), jnp.exp(l2 - m)
    ob = (e0 * o0 + e1 * o1 + e2 * o2) / (e0 + e1 + e2)
    yb = (ob * _silu(gb_ref[0].astype(F32))).astype(BF16)
    y = jnp.dot(ya_ref[0], w_ref[:A_Q_W], preferred_element_type=F32)
    y = y + jnp.dot(yb, w_ref[A_Q_W:], preferred_element_type=F32)
    out_ref[0] = _layer_norm_rows(ALPHA * x_ref[0] + y, g_ref[...], b_ref[...])


def _odd_tail_body(x_ref, yc_ref, of_ref, ob_ref, gd_ref, gn_ref, w_ref, g_ref, b_ref, out_ref):
    od = of_ref[0] + ob_ref[0]
    parts = []
    for h in range(D_HEADS):
        oh = od[:, h * D_HEAD_DIM:(h + 1) * D_HEAD_DIM]
        parts.append(oh * lax.rsqrt(jnp.mean(oh * oh, axis=-1, keepdims=True) + RMS_EPS))
    od = jnp.concatenate(parts, axis=1) * gn_ref[...]
    yd = (od * _silu(gd_ref[0].astype(F32))).astype(BF16)
    y = jnp.dot(yc_ref[0], w_ref[:C_W], preferred_element_type=F32)
    y = y + jnp.dot(yd, w_ref[C_W:], preferred_element_type=F32)
    out_ref[0] = _layer_norm_rows(ALPHA * x_ref[0] + y, g_ref[...], b_ref[...])


def _rows(width, blk=0):
    return pl.BlockSpec((1, ROW_TILE, width), lambda b, i: (b, i, blk))


def _residue_rows(dil, width):
    return pl.BlockSpec((1, dil, ROW_TILE // dil, width), lambda b, i: (b, 0, i, 0))


def _whole(shape):
    return pl.BlockSpec(shape, lambda b, i: (0,) * len(shape))


def _even_tail(x, ya, obs, lses, h_nat, w_out, ln_g, ln_b):
    nb, seq, _ = x.shape
    group_specs = []
    for _, dil in B_PAIRS:
        group_specs += [_rows(B_W) if dil == 1 else _residue_rows(dil, B_W)] * 2
    groups = [a for pair in zip(obs, lses) for a in pair]
    return pl.pallas_call(
        _even_tail_body,
        grid=(nb, seq // ROW_TILE),
        in_specs=[_rows(D_MODEL), _rows(A_Q_W)] + group_specs + [_rows(B_W, N_GB // B_W)]
        + [_whole(w_out.shape), _whole((1, D_MODEL)), _whole((1, D_MODEL))],
        out_specs=_rows(D_MODEL),
        out_shape=jax.ShapeDtypeStruct((nb, seq, D_MODEL), F32),
        scratch_shapes=[pltpu.VMEM((B_W // LANES, ROW_TILE, LANES), F32)] * 4,
        compiler_params=_params(("parallel", "parallel")),
        name="even_tail",
    )(x, ya, *groups, h_nat, w_out, ln_g, ln_b)


def _odd_tail(x, yc, o_f, o_b, hb, gnorm, w_out, ln_g, ln_b):
    nb, seq, _ = x.shape
    return pl.pallas_call(
        _odd_tail_body,
        grid=(nb, seq // ROW_TILE),
        in_specs=[_rows(D_MODEL), _rows(C_W), _rows(D_W), _rows(D_W), _rows(D_W, O_GD // D_W),
                  _whole((1, D_W)), _whole(w_out.shape), _whole((1, D_MODEL)), _whole((1, D_MODEL))],
        out_specs=_rows(D_MODEL),
        out_shape=jax.ShapeDtypeStruct((nb, seq, D_MODEL), F32),
        compiler_params=_params(("parallel", "parallel")),
        name="odd_tail",
    )(x, yc, o_f, o_b, hb, gnorm, w_out, ln_g, ln_b)


def _even_weights(w_in, w_out, sink, t5_table, tiles):
    scale = HEAD_DIM ** -0.5
    qa, ka, va, qb, kb, vb, ga, gb = jnp.split(
        w_in, np.cumsum([A_Q_W, A_KV_W, A_KV_W, N_DIL * B_W, N_DIL * B_W, N_DIL * B_W, A_Q_W])[:7].tolist(), axis=-1)
    grp = lambda w, g: w[:, g * B_W:(g + 1) * B_W]
    w_nat = jnp.concatenate([qa * scale, ka, va, grp(qb, 0) * scale, grp(kb, 0), grp(vb, 0), ga, gb], axis=1)
    w_dil = [jnp.concatenate([grp(qb, g) * scale, grp(kb, g), grp(vb, g)], axis=1) for g in range(1, N_DIL)]
    bias_a = _band_bias(t5_table[:, :A_Q_HEADS], tiles["a"], A_HALF_WINDOW, 1)
    bias_b = []
    for g, (win, dil) in enumerate(B_PAIRS):
        cols = t5_table[:, A_Q_HEADS + g * B_HEADS:A_Q_HEADS + (g + 1) * B_HEADS]
        bias_b.append(_band_bias(cols, tiles["b"][g], win // (2 * dil), dil))
    return dict(w_nat=w_nat.astype(BF16), w_dil=[w.astype(BF16) for w in w_dil], w_out=w_out.astype(BF16),
                sink=sink.astype(F32), bias_a=bias_a, bias_b=bias_b)


def _odd_weights(w_in, w_out, rpb, gnorm):
    scale = HEAD_DIM ** -0.5
    qc, kc, vc, qd, idd, zf, zb, gc, gd = jnp.split(w_in, [C_W * i for i in range(1, 9)], axis=-1)
    return dict(w_bf16=jnp.concatenate([qc * scale, kc, vc, idd, gc, gd], axis=1).astype(BF16),
                w_f32=jnp.concatenate([qd, zf, zb], axis=1).astype(BF16),
                w_out=w_out.astype(BF16), bias_c=_neighbourhood_bias(rpb),
                gnorm=gnorm.reshape(1, D_W).astype(F32))


def _even_layer(x, p, ln_g, ln_b):
    nb, seq, _ = x.shape
    dils = [dil for _, dil in B_PAIRS]
    slabs = _proj(x, [p["w_nat"]] + p["w_dil"], [BF16] * N_DIL, dils)
    h_nat = slabs[0]
    (ya,) = _banded_attention(
        h_nat.reshape(nb, 1, seq, EVEN_NAT_W), p["bias_a"], p["sink"], halo=A_HALF_WINDOW,
        n_heads=A_Q_HEADS, n_kv=A_KV_HEADS, q_off=N_QA, k_off=N_KA, v_off=N_VA, g_off=N_GA,
        out_dtype=BF16, want_lse=False)
    obs, lses = [], []
    for g, (win, dil) in enumerate(B_PAIRS):
        slab = h_nat.reshape(nb, 1, seq, EVEN_NAT_W) if g == 0 else slabs[g]
        offs = (N_QB, N_KB, N_VB) if g == 0 else (R_Q, R_K, R_V)
        o, lse = _banded_attention(
            slab, p["bias_b"][g], None, halo=win // (2 * dil), n_heads=B_HEADS, n_kv=B_HEADS,
            q_off=offs[0], k_off=offs[1], v_off=offs[2], g_off=None, out_dtype=F32, want_lse=True)
        obs.append(o.reshape(nb, seq, B_W) if dil == 1 else o)
        lses.append(lse.reshape(nb, seq, B_W) if dil == 1 else lse)
    return _even_tail(x, ya.reshape(nb, seq, A_Q_W), obs, lses, h_nat, p["w_out"],
                      ln_g.reshape(1, D_MODEL), ln_b.reshape(1, D_MODEL))


def _odd_layer(x, p, lb, ln_g, ln_b):
    nb, seq, _ = x.shape
    assert seq // GRID_W >= 2 * NA_KH
    hb, zq = _proj(x, [p["w_bf16"], p["w_f32"]], [BF16, F32], [1, 1])
    (yc,) = _banded_attention(
        hb.reshape(nb, 1, seq, ODD_BF16_W), p["bias_c"], None, halo=(NA_KH // 2) * GRID_W,
        n_heads=C_HEADS, n_kv=C_HEADS, q_off=O_QC, k_off=O_KC, v_off=O_VC, g_off=O_GC,
        out_dtype=BF16, want_lse=False)
    o_f, o_b = _hgrn_scan(hb, zq, lb)
    return _odd_tail(x, yc.reshape(nb, seq, C_W), o_f, o_b, hb, p["gnorm"], p["w_out"],
                     ln_g.reshape(1, D_MODEL), ln_b.reshape(1, D_MODEL))


def _attention_tiles(seq):
    return dict(a=min(ATTN_TILE, seq), b=[min(ATTN_TILE, seq // dil) for _, dil in B_PAIRS])


def _trunk(x, even, odd, lbs, ln_g, ln_b):
    for layer in range(DEPTH):
        j = layer // 2
        if layer % 2 == 0:
            x = _even_layer(x, even[j], ln_g[layer], ln_b[layer])
        else:
            x = _odd_layer(x, odd[j], lbs[:, layer], ln_g[layer], ln_b[layer])
    return x


def kernel(x_prompt, x_sample, t5_table, w_in_even, sink_a, w_out_even, w_in_odd, rpb_c, lb_d, gnorm_d,
           w_out_odd, ln_g, ln_b):
    lbs = jnp.cumsum(jax.nn.softmax(lb_d.astype(F32), axis=1), axis=1)
    lbs = lbs - lbs[:, :1]
    tiles = _attention_tiles(x_prompt.shape[1])
    assert tiles == _attention_tiles(x_sample.shape[1])
    even = [_even_weights(w_in_even[j], w_out_even[j], sink_a[j], t5_table, tiles) for j in range((DEPTH + 1) // 2)]
    odd = [_odd_weights(w_in_odd[j], w_out_odd[j], rpb_c[j], gnorm_d[j]) for j in range(DEPTH // 2)]
    return tuple(_trunk(x, even, odd, lbs, ln_g, ln_b) for x in (x_prompt, x_sample))
```

```python
import functools
import math

import jax
import jax.numpy as jnp
import numpy as np
from jax import lax
from jax.experimental import pallas as pl
from jax.experimental.pallas import tpu as pltpu

D_MODEL = 1024
DEPTH = 2
HEAD_DIM = 64
A_Q_HEADS = 8
A_KV_HEADS = 2
A_HALF_WINDOW = 128
B_PAIRS = ((128, 1), (512, 4), (2048, 16))
N_DIL = len(B_PAIRS)
B_HEADS = 4
C_HEADS = 8
GRID_W = 64
NA_KH = 8
NA_KW = 16
D_HEADS = 4
D_HEAD_DIM = 128
T5_BUCKETS = 32
T5_MAX_DISTANCE = 1024
ALPHA = (2.0 * DEPTH) ** 0.25
LN_EPS = 1e-5
RMS_EPS = 1e-6

A_Q_W = A_Q_HEADS * HEAD_DIM
A_KV_W = A_KV_HEADS * HEAD_DIM
B_W = B_HEADS * HEAD_DIM
C_W = C_HEADS * HEAD_DIM
D_W = D_HEADS * D_HEAD_DIM

N_QA = 0
N_KA = N_QA + A_Q_W
N_VA = N_KA + A_KV_W
N_QB = N_VA + A_KV_W
N_KB = N_QB + B_W
N_VB = N_KB + B_W
N_GA = N_VB + B_W
N_GB = N_GA + A_Q_W
EVEN_NAT_W = N_GB + B_W
R_Q, R_K, R_V = 0, B_W, 2 * B_W
EVEN_DIL_W = 3 * B_W
O_QC, O_KC, O_VC, O_ID, O_GC, O_GD = (i * C_W for i in range(6))
ODD_BF16_W = 6 * C_W
Z_QD, Z_ZF, Z_ZB = (i * D_W for i in range(3))
ODD_F32_W = 3 * D_W

LANES = 128
MASK_VALUE = -1e30
ROW_TILE = 512
ATTN_STEP = 512
ATTN_SUB_A = 256
ATTN_SUB_B = 128
ATTN_SCORES = 128 * 768
SCAN_CHUNK = 128
PROJ_COLS = 768
VMEM_LIMIT_BYTES = 56 * 1024 * 1024

F32 = jnp.float32
BF16 = jnp.bfloat16


def _params(semantics):
    return pltpu.CompilerParams(dimension_semantics=semantics, vmem_limit_bytes=VMEM_LIMIT_BYTES)


def _silu(g):
    return g * (1.0 / (1.0 + jnp.exp(-g)))


def _dot_nt(a, b):
    return lax.dot_general(a, b, (((1,), (1,)), ((), ())), preferred_element_type=F32)


def _proj_body(x_ref, *refs, dils):
    n_out = len(dils)
    tile, width = x_ref.shape[1], x_ref.shape[2]
    lane_tiles = width // LANES
    xs = {}
    if any(dil > 1 for dil in dils):
        xt_ref = refs[2 * n_out]
        for c in range(lane_tiles):
            xt_ref[c] = x_ref[0, :, c * LANES:(c + 1) * LANES]
    for w_ref, o_ref, dil in zip(refs[:n_out], refs[n_out:2 * n_out], dils):
        sub = tile // dil
        if dil not in xs:
            if dil == 1:
                xv = x_ref[0]
            else:
                xv = jnp.concatenate(
                    [jnp.concatenate([xt_ref[c, pl.ds(r, sub, stride=dil), :] for c in range(lane_tiles)], axis=1)
                     for r in range(dil)], axis=0)
            xs[dil] = xv.astype(BF16)
        n = w_ref.shape[1]
        for c0 in range(0, n, PROJ_COLS):
            c1 = min(c0 + PROJ_COLS, n)
            res = jnp.dot(xs[dil], w_ref[:, c0:c1], preferred_element_type=F32).astype(o_ref.dtype)
            if dil == 1:
                o_ref[0, :, c0:c1] = res
            else:
                for r in range(dil):
                    o_ref[0, r, :, c0:c1] = res[r * sub:(r + 1) * sub]


def _proj(x, ws, dtypes, dils):
    nb, seq, k = x.shape
    assert seq % ROW_TILE == 0
    in_specs = [pl.BlockSpec((1, ROW_TILE, k), lambda b, i: (b, i, 0))]
    in_specs += [pl.BlockSpec(w.shape, lambda b, i: (0, 0)) for w in ws]
    out_specs, out_shape = [], []
    for w, dt, dil in zip(ws, dtypes, dils):
        n = w.shape[1]
        if dil == 1:
            out_specs.append(pl.BlockSpec((1, ROW_TILE, n), lambda b, i: (b, i, 0)))
            out_shape.append(jax.ShapeDtypeStruct((nb, seq, n), dt))
        else:
            assert ROW_TILE % (16 * dil) == 0
            out_specs.append(pl.BlockSpec((1, dil, ROW_TILE // dil, n), lambda b, i: (b, 0, i, 0)))
            out_shape.append(jax.ShapeDtypeStruct((nb, dil, seq // dil, n), dt))
    return pl.pallas_call(
        functools.partial(_proj_body, dils=tuple(dils)),
        grid=(nb, seq // ROW_TILE),
        in_specs=in_specs,
        out_specs=out_specs,
        out_shape=out_shape,
        scratch_shapes=[pltpu.VMEM((k // LANES, ROW_TILE, LANES), F32)] if max(dils) > 1 else [],
        compiler_params=_params(("parallel", "parallel")),
        name="proj",
    )(x, *ws)


def _attn_body(*refs, n_heads, n_kv, sub, halo, n_sub_total, has_gate, has_sink, want_lse):
    it = iter(refs)
    q_ref = next(it)
    kl_ref, kc_ref, kr_ref, vl_ref, vc_ref, vr_ref = (next(it) for _ in range(6))
    g_ref = next(it) if has_gate else None
    bias_ref = next(it)
    sink_ref = next(it) if has_sink else None
    o_ref = next(it)
    lse_ref = next(it) if want_lse else None

    step = q_ref.shape[2]
    n_sub = step // sub
    tk = sub + 2 * halo
    k = jnp.concatenate([kl_ref[0, 0], kc_ref[0, 0], kr_ref[0, 0]], axis=0)
    v = jnp.concatenate([vl_ref[0, 0], vc_ref[0, 0], vr_ref[0, 0]], axis=0)
    span = k.shape[0]
    rows = sub
    while rows * tk > ATTN_SCORES and rows % 16 == 0:
        rows //= 2
    group = n_heads // n_kv
    low_k = lax.broadcasted_iota(jnp.int32, (span, LANES), 1) < HEAD_DIM
    low_q = lax.broadcasted_iota(jnp.int32, (rows, LANES), 1) < HEAD_DIM

    def place(x, src_half, dst_half):
        if src_half != dst_half:
            x = jnp.concatenate([x[:, HEAD_DIM:], x[:, :HEAD_DIM]], axis=1)
        return jnp.where(low_k if dst_half == 0 else ~low_k, x, jnp.zeros_like(x))

    first_sub = pl.program_id(2) * n_sub
    kinds = []
    for j in range(n_sub):
        if n_sub_total == 1:
            kinds.append(0)
        elif 0 < j < n_sub - 1:
            kinds.append(1)
        else:
            g = first_sub + j
            kinds.append(jnp.where(g == 0, 0, jnp.where(g == n_sub_total - 1, 2, 1)))

    ones = [jnp.where(low_k, 1.0, 0.0).astype(BF16), jnp.where(low_k, 0.0, 1.0).astype(BF16)]
    for pair in range(n_heads // 2):
        cols = slice(pair * LANES, (pair + 1) * LANES)
        kmat, vmat = [], []
        for e in range(2):
            kv_head = (2 * pair + e) // group
            tile = slice((kv_head // 2) * LANES, (kv_head // 2 + 1) * LANES)
            kmat.append(place(k[:, tile], kv_head % 2, e))
            vmat.append(jnp.concatenate([place(v[:, tile], kv_head % 2, e), ones[e]], axis=1))
        for j in range(n_sub):
            keys = slice(j * sub, j * sub + tk)
            for r0 in range(0, sub, rows):
                qrows = slice(j * sub + r0, j * sub + r0 + rows)
                qp = q_ref[0, 0, qrows, cols]
                acc, mx = None, []
                for e in range(2):
                    h = 2 * pair + e
                    s = _dot_nt(qp, kmat[e][keys]) + bias_ref[kinds[j], h, r0:r0 + rows, :]
                    m = jnp.max(s, axis=-1, keepdims=True)
                    if has_sink:
                        m = jnp.maximum(m, sink_ref[h])
                    t = jnp.dot(jnp.exp(s - m).astype(BF16), vmat[e][keys], preferred_element_type=F32)
                    acc = t if acc is None else acc + t
                    mx.append(m)
                den = acc[:, LANES:]
                if has_sink:
                    den = den + jnp.where(low_q, jnp.exp(sink_ref[2 * pair] - mx[0]),
                                          jnp.exp(sink_ref[2 * pair + 1] - mx[1]))
                out = acc[:, :LANES] / den
                if has_gate:
                    out = out * _silu(g_ref[0, 0, qrows, cols].astype(F32))
                o_ref[0, 0, qrows, cols] = out.astype(o_ref.dtype)
                if want_lse:
                    lse_ref[0, 0, qrows, cols] = jnp.where(low_q, mx[0], mx[1]) + jnp.log(den)


def _banded_attention(h, bias, sink, *, halo, n_heads, n_kv, q_off, k_off, v_off, g_off, out_dtype, want_lse):
    nb, dil, length, _ = h.shape
    sub = bias.shape[2]
    tq = min(ATTN_STEP, length)
    assert length % tq == 0 and tq % sub == 0 and sub % halo == 0 and n_heads % 2 == 0
    assert bias.shape == (4, n_heads, sub, sub + 2 * halo)
    nblk = length // tq
    n_sub_total = length // sub
    bias = bias[3:] if n_sub_total == 1 else bias[:3]
    per = tq // halo
    nhalo = length // halo
    qw = n_heads * HEAD_DIM
    kw = n_kv * HEAD_DIM
    assert q_off % qw == 0 and k_off % kw == 0 and v_off % kw == 0 and kw % LANES == 0

    in_specs = [pl.BlockSpec((1, 1, tq, qw), lambda b, r, n: (b, r, n, q_off // qw))]
    for off in (k_off, v_off):
        co = off // kw
        in_specs += [
            pl.BlockSpec((1, 1, halo, kw), lambda b, r, n, co=co: (b, r, jnp.maximum(n * per - 1, 0), co)),
            pl.BlockSpec((1, 1, tq, kw), lambda b, r, n, co=co: (b, r, n, co)),
            pl.BlockSpec((1, 1, halo, kw), lambda b, r, n, co=co: (b, r, jnp.minimum((n + 1) * per, nhalo - 1), co)),
        ]
    args = [h] * 7
    if g_off is not None:
        assert g_off % qw == 0
        in_specs.append(pl.BlockSpec((1, 1, tq, qw), lambda b, r, n: (b, r, n, g_off // qw)))
        args.append(h)

    in_specs.append(pl.BlockSpec(bias.shape, lambda b, r, n: (0, 0, 0, 0), pipeline_mode=pl.Buffered(1)))
    args.append(bias)
    if sink is not None:
        in_specs.append(pl.BlockSpec(memory_space=pltpu.SMEM))
        args.append(sink)
    out_specs = [pl.BlockSpec((1, 1, tq, qw), lambda b, r, n: (b, r, n, 0))]
    out_shape = [jax.ShapeDtypeStruct((nb, dil, length, qw), out_dtype)]
    if want_lse:
        out_specs.append(pl.BlockSpec((1, 1, tq, qw), lambda b, r, n: (b, r, n, 0)))
        out_shape.append(jax.ShapeDtypeStruct((nb, dil, length, qw), F32))
    return pl.pallas_call(
        functools.partial(_attn_body, n_heads=n_heads, n_kv=n_kv, sub=sub, halo=halo, n_sub_total=n_sub_total,
                          has_gate=g_off is not None, has_sink=sink is not None, want_lse=want_lse),
        grid=(nb, dil, nblk),
        in_specs=in_specs,
        out_specs=out_specs,
        out_shape=out_shape,
        compiler_params=_params(("parallel", "parallel", "parallel")),
        name="banded_attention",
    )(*args)


def _t5_bucket_table(rel):
    nbk = T5_BUCKETS // 2
    ret = (rel > 0).astype(np.int64) * nbk
    n = np.abs(rel)
    max_exact = nbk // 2
    large = max_exact + (np.log(np.maximum(n, 1) / max_exact) / math.log(T5_MAX_DISTANCE / max_exact)
                         * (nbk - max_exact)).astype(np.int64)
    large = np.minimum(large, nbk - 1)
    return ret + np.where(n < max_exact, n, large)


def _lookup(index, table):
    idx = jnp.asarray(index.astype(np.int32))
    out = jnp.zeros((table.shape[1],) + index.shape, F32)
    for i in range(table.shape[0]):
        out = jnp.where(idx == i, table[i].reshape((-1,) + (1,) * index.ndim), out)
    return out


def _band_bias(t5_cols, tq, halo, dil):
    a = np.arange(tq)[:, None]
    c = np.arange(tq + 2 * halo)[None, :]
    rel = c - halo - a
    band = np.abs(rel) <= halo
    left_ok = np.broadcast_to(c >= halo, band.shape)
    right_ok = np.broadcast_to(c < tq + halo, band.shape)
    masks = np.stack([band & left_ok, band, band & right_ok, band & left_ok & right_ok])
    vals = _lookup(_t5_bucket_table(rel * dil), t5_cols.astype(F32))
    return jnp.where(masks[:, None], vals[None], MASK_VALUE)


def _neighbourhood_bias(rpb):
    rows_per_tile = NA_KH // 2
    big = 1 << 20
    qc = np.arange(GRID_W)[:, None]
    kc = np.arange(GRID_W)[None, :]
    sc = np.clip(qc - NA_KW // 2, 0, GRID_W - NA_KW)
    col_ok = (kc >= sc) & (kc < sc + NA_KW)
    dc = np.clip(kc - qc + NA_KW - 1, 0, 2 * NA_KW - 2)
    heads = rpb.shape[0]
    by_dc = _lookup(dc, jnp.transpose(rpb.astype(F32), (2, 0, 1)).reshape(2 * NA_KW - 1, -1))
    by_dc = by_dc.reshape(heads, 2 * NA_KH - 1, GRID_W, GRID_W)
    blocks = jnp.where(col_ok, by_dc, MASK_VALUE)
    masked = jnp.full((heads, GRID_W, GRID_W), MASK_VALUE, F32)
    tables = []
    for r0, rows in ((0, big), (2 * NA_KH, big), (big - rows_per_tile, big), (2 * NA_KH, big)):
        lines = []
        for i in range(rows_per_tile):
            r = r0 + i
            sr = min(max(r - NA_KH // 2, 0), rows - NA_KH)
            line = []
            for w in range(3 * rows_per_tile):
                kr = r0 - rows_per_tile + w
                line.append(blocks[:, kr - r + NA_KH - 1] if sr <= kr < sr + NA_KH else masked)
            lines.append(jnp.concatenate(line, axis=2))
        tables.append(jnp.concatenate(lines, axis=1))
    return jnp.stack(tables)


def _scan_levels():
    return [1 << i for i in range(SCAN_CHUNK.bit_length() - 1)]


def _scan_constants():
    idx = np.arange(SCAN_CHUNK)
    tris, masks = [], []
    for rev in (False, True):
        tau = (SCAN_CHUNK - 1 - idx) if rev else idx
        tau_t, tau_s = tau[:, None], tau[None, :]
        tris.append(np.tile((tau_s <= tau_t).astype(np.float32), (1, 3)))
        own = [tau_t == tau_s]
        for half in _scan_levels():
            own.append(((tau_t ^ tau_s) < 2 * half) & ((tau_t & half) != 0) & ((tau_s & half) == 0))
        masks.append(np.stack(own).astype(np.float32))
    return jnp.asarray(np.stack(tris), BF16), jnp.asarray(np.stack(masks), F32)


ROLL_LEVELS = 2


def _scan_direction(q_ref, i_ref, z_ref, lb, tri_ref, mask_ref, o_ref, st_ref, b_ref, *, rev):
    chunk = SCAN_CHUNK
    q = q_ref[0]
    f = lb + (1.0 - lb) * (1.0 / (1.0 + jnp.exp(-z_ref[0])))
    kk = 1.0 - f

    def heads(x):
        return [x[:, h * D_HEAD_DIM:(h + 1) * D_HEAD_DIM] for h in range(D_HEADS)]

    g = jnp.log(f) * math.log2(math.e)
    g_hi = g.astype(BF16)
    rest = g - g_hi.astype(F32)
    g_mid = rest.astype(BF16)
    g_lo = (rest - g_mid.astype(F32)).astype(BF16)
    b = jnp.dot(tri_ref[...], jnp.concatenate([g_hi, g_mid, g_lo], axis=0), preferred_element_type=F32)
    b_ref[...] = b

    q_bf = q.astype(BF16)
    k_bf = kk.astype(BF16)
    att = [mask_ref[0] * _dot_nt(qh, kh) for qh, kh in zip(heads(q_bf), heads(k_bf))]

    row = lax.broadcasted_iota(jnp.int32, (chunk, D_W), 0)
    tau = (chunk - 1 - row) if rev else row

    def roll_tau(x, s):
        return pltpu.roll(x, (-s if rev else s) % chunk, axis=0)

    c = b
    for level, half in enumerate(_scan_levels()):
        if level < ROLL_LEVELS:
            first = (tau & half) == 0
            ref_b = jnp.where(first, c, roll_tau(c, half))
            expo = -jnp.abs(b - ref_b)
            if level + 1 < ROLL_LEVELS:
                c = jnp.where(first, roll_tau(c, -half), c)
        else:
            pieces = []
            for r0 in range(0, chunk, 2 * half):
                ref_row = (r0 + half) if rev else (r0 + half - 1)
                ref_b = b_ref[ref_row:ref_row + 1, :]
                blk = b[r0:r0 + 2 * half]
                if half % 8 == 0:
                    top, bot = blk[:half], blk[half:]
                    pieces += [top - ref_b, ref_b - bot] if rev else [ref_b - top, bot - ref_b]
                else:
                    pieces.append(-jnp.abs(blk - ref_b))
            expo = jnp.concatenate(pieces, axis=0)
        e = jnp.exp2(expo).astype(BF16)
        for h, (qh, kh) in enumerate(zip(heads(q_bf * e), heads(k_bf * e))):
            att[h] = att[h] + mask_ref[1 + level] * _dot_nt(qh, kh)
        yield

    b_last = b[0:1] if rev else b[chunk - 1:chunk]
    q_in = heads(q_bf * jnp.exp2(b).astype(BF16))
    k_out = heads(k_bf * jnp.exp2(b_last - b).astype(BF16))
    carry = jnp.exp2(b_last)
    ivh = heads(i_ref[0])
    for h in range(D_HEADS):
        sl = slice(h * D_HEAD_DIM, (h + 1) * D_HEAD_DIM)
        st = st_ref[h]
        o_ref[0, :, sl] = _dot_nt(q_in[h], st.astype(BF16)) + jnp.dot(att[h].astype(BF16), ivh[h],
                                                                      preferred_element_type=F32)
        upd = lax.dot_general(ivh[h], k_out[h], (((0,), (0,)), ((), ())), preferred_element_type=F32)
        st_ref[h] = st * carry[:, sl] + upd


def _hgrn_body(qf_ref, qb_ref, if_ref, ib_ref, zf_ref, zb_ref, lb_ref, tri_ref, mask_ref,
               of_ref, ob_ref, st_ref, b_ref):
    @pl.when(pl.program_id(1) == 0)
    def _():
        st_ref[...] = jnp.zeros_like(st_ref)

    scans = [
        _scan_direction(qf_ref, if_ref, zf_ref, lb_ref[0:1], tri_ref.at[0], mask_ref.at[0], of_ref,
                        st_ref.at[0], b_ref.at[0], rev=False),
        _scan_direction(qb_ref, ib_ref, zb_ref, lb_ref[1:2], tri_ref.at[1], mask_ref.at[1], ob_ref,
                        st_ref.at[1], b_ref.at[1], rev=True),
    ]
    while scans:
        scans = [s for s in scans if next(s, True) is None]


def _hgrn_scan(hb, zq, lb):
    nb, seq, _ = hb.shape
    nchunk = seq // SCAN_CHUNK
    tri, masks = _scan_constants()

    def chunk_spec(col, rev):
        if rev:
            return pl.BlockSpec((1, SCAN_CHUNK, D_W), lambda b, n: (b, nchunk - 1 - n, col))
        return pl.BlockSpec((1, SCAN_CHUNK, D_W), lambda b, n: (b, n, col))

    def whole(a):
        return pl.BlockSpec(a.shape, lambda b, n: (0,) * a.ndim)

    return pl.pallas_call(
        _hgrn_body,
        grid=(nb, nchunk),
        in_specs=[chunk_spec(Z_QD // D_W, False), chunk_spec(Z_QD // D_W, True),
                  chunk_spec(O_ID // D_W, False), chunk_spec(O_ID // D_W, True),
                  chunk_spec(Z_ZF // D_W, False), chunk_spec(Z_ZB // D_W, True),
                  whole(lb), whole(tri), whole(masks)],
        out_specs=[chunk_spec(0, False), chunk_spec(0, True)],
        out_shape=[jax.ShapeDtypeStruct((nb, seq, D_W), F32)] * 2,
        scratch_shapes=[pltpu.VMEM((2, D_HEADS, D_HEAD_DIM, D_HEAD_DIM), F32),
                        pltpu.VMEM((2, SCAN_CHUNK, D_W), F32)],
        compiler_params=_params(("parallel", "arbitrary")),
        name="hgrn_scan",
    )(zq, zq, hb, hb, zq, zq, lb, tri, masks)


def _layer_norm_rows(z, g, b):
    mu = jnp.mean(z, axis=-1, keepdims=True)
    zc = z - mu
    var = jnp.mean(zc * zc, axis=-1, keepdims=True)
    return zc * lax.rsqrt(var + LN_EPS) * g + b


def _natural_order(blk_ref, scr_ref):
    dil, sub = blk_ref.shape[1], blk_ref.shape[2]
    lane_tiles = scr_ref.shape[0]
    for r in range(dil):
        for c in range(lane_tiles):
            scr_ref[c, pl.ds(r, sub, stride=dil), :] = blk_ref[0, r, :, c * LANES:(c + 1) * LANES]
    return jnp.concatenate([scr_ref[c] for c in range(lane_tiles)], axis=1)


def _even_tail_body(x_ref, ya_ref, o0_ref, l0_ref, o1_ref, l1_ref, o2_ref, l2_ref, gb_ref, w_ref,
                    g_ref, b_ref, out_ref, so1_ref, sl1_ref, so2_ref, sl2_ref):
    o0, l0 = o0_ref[0], l0_ref[0]
    o1, l1 = _natural_order(o1_ref, so1_ref), _natural_order(l1_ref, sl1_ref)
    o2, l2 = _natural_order(o2_ref, so2_ref), _natural_order(l2_ref, sl2_ref)
    m = jnp.maximum(jnp.maximum(l0, l1), l2)
    e0, e1, e2 = jnp.exp(l0 - m), jnp.exp(l1 - m), jnp.exp(l2 - m)
    ob = (e0 * o0 + e1 * o1 + e2 * o2) / (e0 + e1 + e2)
    yb = (ob * _silu(gb_ref[0].astype(F32))).astype(BF16)
    y = jnp.dot(ya_ref[0], w_ref[:A_Q_W], preferred_element_type=F32)
    y = y + jnp.dot(yb, w_ref[A_Q_W:], preferred_element_type=F32)
    out_ref[0] = _layer_norm_rows(ALPHA * x_ref[0] + y, g_ref[...], b_ref[...])


def _odd_tail_body(x_ref, yc_ref, of_ref, ob_ref, gd_ref, gn_ref, w_ref, g_ref, b_ref, out_ref):
    od = of_ref[0] + ob_ref[0]
    parts = []
    for h in range(D_HEADS):
        oh = od[:, h * D_HEAD_DIM:(h + 1) * D_HEAD_DIM]
        parts.append(oh * lax.rsqrt(jnp.mean(oh * oh, axis=-1, keepdims=True) + RMS_EPS))
    od = jnp.concatenate(parts, axis=1) * gn_ref[...]
    yd = (od * _silu(gd_ref[0].astype(F32))).astype(BF16)
    y = jnp.dot(yc_ref[0], w_ref[:C_W], preferred_element_type=F32)
    y = y + jnp.dot(yd, w_ref[C_W:], preferred_element_type=F32)
    out_ref[0] = _layer_norm_rows(ALPHA * x_ref[0] + y, g_ref[...], b_ref[...])


def _rows(width, blk=0):
    return pl.BlockSpec((1, ROW_TILE, width), lambda b, i: (b, i, blk))


def _residue_rows(dil, width):
    return pl.BlockSpec((1, dil, ROW_TILE // dil, width), lambda b, i: (b, 0, i, 0))


def _whole(shape):
    return pl.BlockSpec(shape, lambda b, i: (0,) * len(shape))


def _even_tail(x, ya, obs, lses, h_nat, w_out, ln_g, ln_b):
    nb, seq, _ = x.shape
    group_specs = []
    for _, dil in B_PAIRS:
        group_specs += [_rows(B_W) if dil == 1 else _residue_rows(dil, B_W)] * 2
    groups = [a for pair in zip(obs, lses) for a in pair]
    return pl.pallas_call(
        _even_tail_body,
        grid=(nb, seq // ROW_TILE),
        in_specs=[_rows(D_MODEL), _rows(A_Q_W)] + group_specs + [_rows(B_W, N_GB // B_W)]
        + [_whole(w_out.shape), _whole((1, D_MODEL)), _whole((1, D_MODEL))],
        out_specs=_rows(D_MODEL),
        out_shape=jax.ShapeDtypeStruct((nb, seq, D_MODEL), F32),
        scratch_shapes=[pltpu.VMEM((B_W // LANES, ROW_TILE, LANES), F32)] * 4,
        compiler_params=_params(("parallel", "parallel")),
        name="even_tail",
    )(x, ya, *groups, h_nat, w_out, ln_g, ln_b)


def _odd_tail(x, yc, o_f, o_b, hb, gnorm, w_out, ln_g, ln_b):
    nb, seq, _ = x.shape
    return pl.pallas_call(
        _odd_tail_body,
        grid=(nb, seq // ROW_TILE),
        in_specs=[_rows(D_MODEL), _rows(C_W), _rows(D_W), _rows(D_W), _rows(D_W, O_GD // D_W),
                  _whole((1, D_W)), _whole(w_out.shape), _whole((1, D_MODEL)), _whole((1, D_MODEL))],
        out_specs=_rows(D_MODEL),
        out_shape=jax.ShapeDtypeStruct((nb, seq, D_MODEL), F32),
        compiler_params=_params(("parallel", "parallel")),
        name="odd_tail",
    )(x, yc, o_f, o_b, hb, gnorm, w_out, ln_g, ln_b)


def _even_weights(w_in, w_out, sink, t5_table, tiles):
    scale = HEAD_DIM ** -0.5
    qa, ka, va, qb, kb, vb, ga, gb = jnp.split(
        w_in, np.cumsum([A_Q_W, A_KV_W, A_KV_W, N_DIL * B_W, N_DIL * B_W, N_DIL * B_W, A_Q_W])[:7].tolist(), axis=-1)
    grp = lambda w, g: w[:, g * B_W:(g + 1) * B_W]
    w_nat = jnp.concatenate([qa * scale, ka, va, grp(qb, 0) * scale, grp(kb, 0), grp(vb, 0), ga, gb], axis=1)
    w_dil = [jnp.concatenate([grp(qb, g) * scale, grp(kb, g), grp(vb, g)], axis=1) for g in range(1, N_DIL)]
    bias_a = _band_bias(t5_table[:, :A_Q_HEADS], tiles["a"], A_HALF_WINDOW, 1)
    bias_b = []
    for g, (win, dil) in enumerate(B_PAIRS):
        cols = t5_table[:, A_Q_HEADS + g * B_HEADS:A_Q_HEADS + (g + 1) * B_HEADS]
        bias_b.append(_band_bias(cols, tiles["b"][g], win // (2 * dil), dil))
    return dict(w_nat=w_nat.astype(BF16), w_dil=[w.astype(BF16) for w in w_dil], w_out=w_out.astype(BF16),
                sink=sink.astype(F32), bias_a=bias_a, bias_b=bias_b)


def _odd_weights(w_in, w_out, rpb, gnorm):
    scale = HEAD_DIM ** -0.5
    qc, kc, vc, qd, idd, zf, zb, gc, gd = jnp.split(w_in, [C_W * i for i in range(1, 9)], axis=-1)
    return dict(w_bf16=jnp.concatenate([qc * scale, kc, vc, idd, gc, gd], axis=1).astype(BF16),
                w_f32=jnp.concatenate([qd, zf, zb], axis=1).astype(BF16),
                w_out=w_out.astype(BF16), bias_c=_neighbourhood_bias(rpb),
                gnorm=gnorm.reshape(1, D_W).astype(F32))


def _even_layer(x, p, ln_g, ln_b):
    nb, seq, _ = x.shape
    dils = [dil for _, dil in B_PAIRS]
    slabs = _proj(x, [p["w_nat"]] + p["w_dil"], [BF16] * N_DIL, dils)
    h_nat = slabs[0]
    (ya,) = _banded_attention(
        h_nat.reshape(nb, 1, seq, EVEN_NAT_W), p["bias_a"], p["sink"], halo=A_HALF_WINDOW,
        n_heads=A_Q_HEADS, n_kv=A_KV_HEADS, q_off=N_QA, k_off=N_KA, v_off=N_VA, g_off=N_GA,
        out_dtype=BF16, want_lse=False)
    obs, lses = [], []
    for g, (win, dil) in enumerate(B_PAIRS):
        slab = h_nat.reshape(nb, 1, seq, EVEN_NAT_W) if g == 0 else slabs[g]
        offs = (N_QB, N_KB, N_VB) if g == 0 else (R_Q, R_K, R_V)
        o, lse = _banded_attention(
            slab, p["bias_b"][g], None, halo=win // (2 * dil), n_heads=B_HEADS, n_kv=B_HEADS,
            q_off=offs[0], k_off=offs[1], v_off=offs[2], g_off=None, out_dtype=F32, want_lse=True)
        obs.append(o.reshape(nb, seq, B_W) if dil == 1 else o)
        lses.append(lse.reshape(nb, seq, B_W) if dil == 1 else lse)
    return _even_tail(x, ya.reshape(nb, seq, A_Q_W), obs, lses, h_nat, p["w_out"],
                      ln_g.reshape(1, D_MODEL), ln_b.reshape(1, D_MODEL))


def _odd_layer(x, p, lb, ln_g, ln_b):
    nb, seq, _ = x.shape
    assert seq // GRID_W >= 2 * NA_KH
    hb, zq = _proj(x, [p["w_bf16"], p["w_f32"]], [BF16, F32], [1, 1])
    (yc,) = _banded_attention(
        hb.reshape(nb, 1, seq, ODD_BF16_W), p["bias_c"], None, halo=(NA_KH // 2) * GRID_W,
        n_heads=C_HEADS, n_kv=C_HEADS, q_off=O_QC, k_off=O_KC, v_off=O_VC, g_off=O_GC,
        out_dtype=BF16, want_lse=False)
    o_f, o_b = _hgrn_scan(hb, zq, lb)
    return _odd_tail(x, yc.reshape(nb, seq, C_W), o_f, o_b, hb, p["gnorm"], p["w_out"],
                     ln_g.reshape(1, D_MODEL), ln_b.reshape(1, D_MODEL))


def _attention_tiles(seq):
    return dict(a=min(ATTN_SUB_A, seq), b=[min(ATTN_SUB_B, seq // dil) for _, dil in B_PAIRS])


def _trunk(x, even, odd, lbs, ln_g, ln_b):
    for layer in range(DEPTH):
        j = layer // 2
        if layer % 2 == 0:
            x = _even_layer(x, even[j], ln_g[layer], ln_b[layer])
        else:
            x = _odd_layer(x, odd[j], lbs[:, layer], ln_g[layer], ln_b[layer])
    return x


def kernel(x_prompt, x_sample, t5_table, w_in_even, sink_a, w_out_even, w_in_odd, rpb_c, lb_d, gnorm_d,
           w_out_odd, ln_g, ln_b):
    lbs = jnp.cumsum(jax.nn.softmax(lb_d.astype(F32), axis=1), axis=1)
    lbs = lbs - lbs[:, :1]
    tiles = _attention_tiles(x_prompt.shape[1])
    assert tiles == _attention_tiles(x_sample.shape[1])
    even = [_even_weights(w_in_even[j], w_out_even[j], sink_a[j], t5_table, tiles) for j in range((DEPTH + 1) // 2)]
    odd = [_odd_weights(w_in_odd[j], w_out_odd[j], rpb_c[j], gnorm_d[j]) for j in range(DEPTH // 2)]
    return tuple(_trunk(x, even, odd, lbs, ln_g, ln_b) for x in (x_prompt, x_sample))
```
